```python
import math
import jax
import jax.numpy as jnp
from jax import lax
import numpy as np

D_MODEL = 1024
BATCH = 4
SEQ = 8192
DEPTH = 4

HEAD_DIM = 64
N_HEADS_TOTAL = D_MODEL // HEAD_DIM
N_MEM_HEADS = 4
N_SELF_HEADS = N_HEADS_TOTAL - N_MEM_HEADS
SELF_WIDTH = N_SELF_HEADS * HEAD_DIM
MEM_WIDTH = N_MEM_HEADS * HEAD_DIM
IN_COLS = 3 * SELF_WIDTH + MEM_WIDTH
N_MEM = 256
N_MIXERS = 2
N_MOBA_LAYERS = (DEPTH + 1) // 2
N_FOX_LAYERS = DEPTH // 2
MOBA_BLOCK = 256
MOBA_TOPK = 3
MOBA_Q_CHUNK = 16
FOX_Q_BLOCK = 128
FOX_GATE_BIAS_INIT = 2.0
N_EXPERTS = 32
TOP_K = 4
D_FF = D_MODEL
SWIGLU_LIMIT = 7.0
SWIGLU_ALPHA = 1.702
MOE_BLOCK = 512
DEEPNORM_ALPHA = (2 * DEPTH) ** 0.25
DEEPNORM_BETA = (8 * DEPTH) ** -0.25
LN_EPS = 1e-5

kernel_name = 'hybrid_moba_fox_memxattn_moe_deepnorm'


def _alibi_slopes(n):
    def pow2(m):
        start = 2.0 ** (-(2.0 ** -(math.log2(m) - 3)))
        return [start * start ** i for i in range(m)]
    if math.log2(n).is_integer():
        s = pow2(n)
    else:
        c = 2 ** math.floor(math.log2(n))
        s = pow2(c) + pow2(2 * c)[0::2][: n - c]
    return jnp.asarray(np.array(s, dtype=np.float32))


def layer_norm(x, g, b):
    xf = x.astype(jnp.float32)
    mu = jnp.mean(xf, axis=-1, keepdims=True)
    var = jnp.mean(jnp.square(xf - mu), axis=-1, keepdims=True)
    y = (xf - mu) * lax.rsqrt(var + LN_EPS) * g.astype(jnp.float32) + b.astype(jnp.float32)
    return y.astype(x.dtype)


def split_heads(t, n):
    b, s, _ = t.shape
    return t.reshape(b, s, n, HEAD_DIM).transpose(0, 2, 1, 3)


def moba_attention(q, k, v):
    B, H, T, dh = q.shape
    L = MOBA_BLOCK
    nb = -(-T // L)
    tp = nb * L
    pad = ((0, 0), (0, 0), (0, tp - T), (0, 0))
    q = jnp.pad(q, pad)
    k = jnp.pad(k, pad)
    v = jnp.pad(v, pad)
    kb = k.reshape(B, H, nb, L, dh)
    vb = v.reshape(B, H, nb, L, dh)
    kbar = jnp.mean(kb.astype(jnp.float32), axis=3)
    tpos = jnp.arange(tp)
    qblk = tpos // L
    gate = jnp.einsum('bhtd,bhnd->bhtn', q.astype(jnp.float32), kbar)
    past = jnp.arange(nb)[None, :] < qblk[:, None]
    gate = jnp.where(past, gate, -jnp.inf)
    ks = min(MOBA_TOPK, nb)
    _, sel = lax.top_k(gate, ks)
    sel_ok = sel < qblk[:, None]
    slopes = _alibi_slopes(H)[None, :, None, None]
    scale = dh ** -0.5
    gather = jax.vmap(jax.vmap(lambda blocks, ids: blocks[ids]))
    offs = jnp.arange(L)
    QC = MOBA_Q_CHUNK

    def chunk(i):
        t0 = i * QC
        qc = lax.dynamic_slice_in_dim(q, t0, QC, axis=2)
        tq = t0 + jnp.arange(QC)
        own = t0 // L
        k_own = lax.dynamic_index_in_dim(kb, own, axis=2, keepdims=False)
        v_own = lax.dynamic_index_in_dim(vb, own, axis=2, keepdims=False)
        ids = lax.dynamic_slice_in_dim(sel, t0, QC, axis=2)
        ok = lax.dynamic_slice_in_dim(sel_ok, t0, QC, axis=2)
        k_sel = gather(kb, ids)
        v_sel = gather(vb, ids)
        d_own = tq[:, None] - (own * L + offs)[None, :]
        s_own = jnp.einsum('bhqd,bhkd->bhqk', qc, k_own, preferred_element_type=jnp.float32) * scale
        s_own = jnp.where(d_own >= 0, s_own - slopes * d_own.astype(jnp.float32), -jnp.inf)
        d_sel = tq[None, None, :, None, None] - (ids[..., None] * L + offs)
        s_sel = jnp.einsum('bhqd,bhqjkd->bhqjk', qc, k_sel, preferred_element_type=jnp.float32) * scale
        s_sel = jnp.where(ok[..., None], s_sel - slopes[..., None] * d_sel.astype(jnp.float32), -jnp.inf)
        s = jnp.concatenate([s_own, s_sel.reshape(B, H, QC, ks * L)], axis=-1)
        p = jax.nn.softmax(s, axis=-1).astype(v.dtype)
        p_own = p[..., :L]
        p_sel = p[..., L:].reshape(B, H, QC, ks, L)
        return (jnp.einsum('bhqk,bhkd->bhqd', p_own, v_own)
                + jnp.einsum('bhqjk,bhqjkd->bhqd', p_sel, v_sel))

    out = lax.map(chunk, jnp.arange(tp // QC))
    out = out.transpose(1, 2, 0, 3, 4).reshape(B, H, tp, dh)
    return out[:, :, :T]


def fox_attention(q, k, v, fg_logit):
    B, H, T, dh = q.shape
    c = jnp.cumsum(jax.nn.log_sigmoid(fg_logit.astype(jnp.float32)), axis=-1)
    scale = dh ** -0.5
    kpos = jnp.arange(T)
    QB = FOX_Q_BLOCK

    def block(i):
        t0 = i * QB
        qb = lax.dynamic_slice_in_dim(q, t0, QB, axis=2)
        cq = lax.dynamic_slice_in_dim(c, t0, QB, axis=2)
        tq = t0 + jnp.arange(QB)
        s = jnp.einsum('bhqd,bhkd->bhqk', qb, k, preferred_element_type=jnp.float32) * scale
        s = s + (cq[..., None] - c[:, :, None, :])
        s = jnp.where(kpos[None, :] <= tq[:, None], s, -jnp.inf)
        p = jax.nn.softmax(s, axis=-1).astype(v.dtype)
        return jnp.einsum('bhqk,bhkd->bhqd', p, v)

    out = lax.map(block, jnp.arange(T // QB))
    return out.transpose(1, 2, 0, 3, 4).reshape(B, H, T, dh)


def memory_attention(q, mk, mv):
    s = jnp.einsum('bhqd,bhmd->bhqm', q, mk, preferred_element_type=jnp.float32) * (HEAD_DIM ** -0.5)
    p = jax.nn.softmax(s, axis=-1).astype(mv.dtype)
    return jnp.einsum('bhqm,bhmd->bhqd', p, mv)


def moe(h, router_w, router_b, w_gu, b_gu, w_dn, b_dn):
    B, T, D = h.shape
    N = B * T
    xt = h.reshape(N, D)
    logits = (xt @ router_w).astype(jnp.float32) + router_b.astype(jnp.float32)
    top_v, top_i = lax.top_k(logits, TOP_K)
    gates = jax.nn.softmax(top_v, axis=-1)
    A = N * TOP_K
    flat_e = top_i.reshape(A)
    flat_tok = jnp.repeat(jnp.arange(N), TOP_K)
    order = jnp.argsort(flat_e)
    nblk = A // MOE_BLOCK
    se_blk = flat_e[order].reshape(nblk, MOE_BLOCK)
    stok_blk = flat_tok[order].reshape(nblk, MOE_BLOCK)
    sg_blk = gates.reshape(A)[order].reshape(nblk, MOE_BLOCK)
    e_first = se_blk[:, 0]
    e_last = se_blk[:, -1]
    cnt = e_last - e_first + 1
    ends = jnp.cumsum(cnt)
    starts = ends - cnt
    n_items = nblk + N_EXPERTS - 1
    it = jnp.arange(n_items)
    item_ok = it < ends[-1]
    item_blk = jnp.minimum(jnp.searchsorted(ends, it, side='right'), nblk - 1)
    item_e = jnp.clip(e_first[item_blk] + it - starts[item_blk], 0, N_EXPERTS - 1)

    def run(args):
        b, e, ok = args
        xb = xt[stok_blk[b]]
        gu = xb @ w_gu[e] + b_gu[e]
        g = jnp.minimum(gu[:, :D_FF], SWIGLU_LIMIT)
        u = jnp.clip(gu[:, D_FF:], -SWIGLU_LIMIT, SWIGLU_LIMIT)
        a = g * jax.nn.sigmoid(SWIGLU_ALPHA * g) * (u + 1)
        y = a @ w_dn[e] + b_dn[e]
        wt = jnp.where((se_blk[b] == e) & ok, sg_blk[b], 0.0)
        return y * wt.astype(y.dtype)[:, None]

    ys = lax.map(run, (item_blk, item_e, item_ok))
    out = jnp.zeros((N, D), ys.dtype).at[stok_blk[item_blk].reshape(-1)].add(ys.reshape(-1, D))
    return out.reshape(B, T, D)


def setup_inputs(seed: int = 0) -> dict:
    key = jax.random.key(seed)
    ks = jax.random.split(key, 19)
    nrm = jax.random.normal
    s_in = D_MODEL ** -0.5
    return {
        'x': nrm(ks[0], (BATCH, SEQ, D_MODEL), jnp.float32),
        'mem': nrm(ks[1], (BATCH, N_MEM, D_MODEL), jnp.float32),
        'w_in_moba': nrm(ks[2], (N_MOBA_LAYERS, D_MODEL, IN_COLS), jnp.float32) * s_in,
        'w_in_fox': nrm(ks[3], (N_FOX_LAYERS, D_MODEL, IN_COLS + N_SELF_HEADS), jnp.float32) * s_in,
        'b_fgate': FOX_GATE_BIAS_INIT + 0.1 * nrm(ks[4], (N_FOX_LAYERS, N_SELF_HEADS), jnp.float32),
        'w_mem_kv': nrm(ks[5], (D_MODEL, 2 * MEM_WIDTH), jnp.float32) * s_in,
        'w_o': nrm(ks[6], (DEPTH, D_MODEL, D_MODEL), jnp.float32) * (s_in * DEEPNORM_BETA),
        'ln1_g': 1.0 + 0.02 * nrm(ks[7], (DEPTH, D_MODEL), jnp.float32),
        'ln1_b': 0.02 * nrm(ks[8], (DEPTH, D_MODEL), jnp.float32),
        'router_w': nrm(ks[9], (DEPTH, D_MODEL, N_EXPERTS), jnp.float32) * s_in,
        'router_b': 0.01 * nrm(ks[10], (DEPTH, N_EXPERTS), jnp.float32),
        'w_gate_up': nrm(ks[11], (DEPTH, N_EXPERTS, D_MODEL, 2 * D_FF), jnp.float32) * s_in,
        'b_gate_up': 0.01 * nrm(ks[12], (DEPTH, N_EXPERTS, 2 * D_FF), jnp.float32),
        'w_down': nrm(ks[13], (DEPTH, N_EXPERTS, D_FF, D_MODEL), jnp.float32) * (D_FF ** -0.5 * DEEPNORM_BETA),
        'b_down': 0.01 * nrm(ks[14], (DEPTH, N_EXPERTS, D_MODEL), jnp.float32),
        'ln2_g': 1.0 + 0.02 * nrm(ks[15], (DEPTH, D_MODEL), jnp.float32),
        'ln2_b': 0.02 * nrm(ks[16], (DEPTH, D_MODEL), jnp.float32),
    }


def reference(x, mem, w_in_moba, w_in_fox, b_fgate, w_mem_kv, w_o, ln1_g, ln1_b,
              router_w, router_b, w_gate_up, b_gate_up, w_down, b_down, ln2_g, ln2_b):
    B, T, D = x.shape
    mkv = mem @ w_mem_kv
    mk = split_heads(mkv[..., :MEM_WIDTH], N_MEM_HEADS)
    mv = split_heads(mkv[..., MEM_WIDTH:], N_MEM_HEADS)
    h = x
    for i in range(DEPTH):
        j = i // N_MIXERS
        is_moba = (i % N_MIXERS) == 0
        proj = h @ (w_in_moba[j] if is_moba else w_in_fox[j])
        q_s = split_heads(proj[..., :SELF_WIDTH], N_SELF_HEADS)
        k_s = split_heads(proj[..., SELF_WIDTH:2 * SELF_WIDTH], N_SELF_HEADS)
        v_s = split_heads(proj[..., 2 * SELF_WIDTH:3 * SELF_WIDTH], N_SELF_HEADS)
        q_m = split_heads(proj[..., 3 * SELF_WIDTH:IN_COLS], N_MEM_HEADS)
        if is_moba:
            o_self = moba_attention(q_s, k_s, v_s)
        else:
            fg = (proj[..., IN_COLS:] + b_fgate[j]).transpose(0, 2, 1)
            o_self = fox_attention(q_s, k_s, v_s, fg)
        o_mem = memory_attention(q_m, mk, mv)
        heads = jnp.concatenate([o_self, o_mem], axis=1)
        merged = heads.transpose(0, 2, 1, 3).reshape(B, T, D)
        h = layer_norm(DEEPNORM_ALPHA * h + merged @ w_o[i], ln1_g[i], ln1_b[i])
        f = moe(h, router_w[i], router_b[i], w_gate_up[i], b_gate_up[i], w_down[i], b_down[i])
        h = layer_norm(DEEPNORM_ALPHA * h + f, ln2_g[i], ln2_b[i])
    return h
```

```python
import functools
import math

import numpy as np
import jax
import jax.numpy as jnp
from jax import lax
from jax.experimental import pallas as pl
from jax.experimental.pallas import tpu as pltpu

D_MODEL = 1024
HEAD_DIM = 64
N_SELF_HEADS = 12
N_MEM_HEADS = 4
SELF_WIDTH = N_SELF_HEADS * HEAD_DIM
MEM_WIDTH = N_MEM_HEADS * HEAD_DIM
IN_COLS = 3 * SELF_WIDTH + MEM_WIDTH
MOBA_BLOCK = 256
MOBA_TOPK = 3
N_EXPERTS = 32
TOP_K = 4
D_FF = D_MODEL
SWIGLU_LIMIT = 7.0
SWIGLU_ALPHA = 1.702
LN_EPS = 1e-5

LANES = 128
HEAD_BLOCK = LANES
SELF_AUG = N_SELF_HEADS * HEAD_BLOCK
MEM_AUG = N_MEM_HEADS * HEAD_BLOCK
AUX_ONES = 36
QK_SCALE = HEAD_DIM ** -0.5
MASK_VALUE = -1e30
VMEM_LIMIT = 56 * 1024 * 1024

_MXU = jnp.bfloat16
_F32 = jnp.float32


def _dot(a, b):
    return jnp.dot(a, b, preferred_element_type=_F32)


def _dot_nt(a, b):
    return lax.dot_general(a, b, (((1,), (1,)), ((), ())), preferred_element_type=_F32)


def _split3(x):
    p1 = x.astype(_MXU)
    r1 = x - p1.astype(_F32)
    p2 = r1.astype(_MXU)
    r2 = r1 - p2.astype(_F32)
    return p1, p2, r2.astype(_MXU)


def _layer_norm(y, g, b):
    mu = jnp.mean(y, axis=-1, keepdims=True)
    yc = y - mu
    var = jnp.mean(yc * yc, axis=-1, keepdims=True)
    return yc * lax.rsqrt(var + LN_EPS) * g + b


def _params(*sem):
    return pltpu.CompilerParams(dimension_semantics=sem, vmem_limit_bytes=VMEM_LIMIT)


def _const_spec(shape):
    nd = len(shape)
    return pl.BlockSpec(shape, lambda *_: (0,) * nd)


def _proj_kernel(*refs, n_out, has_aux, with_kbar):
    x_ref, aux_ref = refs[0], refs[1]
    w_refs = refs[2:2 + n_out]
    s_refs = refs[2 + n_out:2 + 2 * n_out]
    o_refs = refs[2 + 2 * n_out:2 + 3 * n_out]
    x = x_ref[...].astype(_MXU)
    aux = aux_ref[...]
    for i in range(n_out):
        acc = _dot(x, w_refs[i][...])
        if has_aux[i]:
            acc = acc + _dot(aux, s_refs[i][...])
        o_refs[i][...] = acc.astype(o_refs[i].dtype)
        if with_kbar and i == 1:
            kbar_ref = refs[2 + 3 * n_out]
            lane = lax.broadcasted_iota(jnp.int32, (1, acc.shape[1]), 1)
            kbar = jnp.mean(acc, axis=0, keepdims=True)
            kbar_ref[0] = jnp.where(lane % HEAD_BLOCK < HEAD_DIM, kbar, 0.0)


def _proj(x, aux, ws, ss, *, with_kbar=False, tm=256):
    m, kdim = x.shape
    n_out = len(ws)
    has_aux = tuple(s is not None for s in ss)
    ss = [s if s is not None else jnp.zeros((LANES, LANES), _MXU) for s in ss]
    out_shape = [jax.ShapeDtypeStruct((m, w.shape[1]), _MXU) for w in ws]
    out_specs = [pl.BlockSpec((tm, w.shape[1]), lambda i: (i, 0)) for w in ws]
    if with_kbar:
        assert tm == MOBA_BLOCK
        out_shape.append(jax.ShapeDtypeStruct((m // tm, 1, ws[1].shape[1]), _F32))
        out_specs.append(pl.BlockSpec((1, 1, ws[1].shape[1]), lambda i: (i, 0, 0)))
    in_specs = [pl.BlockSpec((tm, kdim), lambda i: (i, 0)),
                pl.BlockSpec((tm, LANES), lambda i: (i, 0))]
    in_specs += [_const_spec(w.shape) for w in ws]
    in_specs += [_const_spec(s.shape) for s in ss]
    return pl.pallas_call(
        functools.partial(_proj_kernel, n_out=n_out, has_aux=has_aux, with_kbar=with_kbar),
        out_shape=out_shape,
        grid=(m // tm,),
        in_specs=in_specs,
        out_specs=out_specs,
        compiler_params=_params("parallel"),
        name="proj",
    )(x, aux, *ws, *ss)


def _fgate_kernel(x_ref, w_ref, b_ref, tri_ref, place_ref, ones_ref, aux_ref, carry_ref):
    @pl.when(pl.program_id(1) == 0)
    def _():
        carry_ref[...] = jnp.zeros_like(carry_ref)

    z = _dot(x_ref[...].astype(_MXU), w_ref[...]) + b_ref[...]
    ls = jnp.minimum(z, 0.0) - jnp.log(1.0 + jnp.exp(-jnp.abs(z)))
    tri = tri_ref[...]
    l1, l2, l3 = _split3(ls)
    c = _dot(tri, l1) + _dot(tri, l2) + _dot(tri, l3) + carry_ref[...]
    carry_ref[...] = c[c.shape[0] - 1:c.shape[0], :]
    c1, c2, c3 = _split3(c)
    aux = _dot(c1, place_ref[0]) + _dot(c2, place_ref[1]) + _dot(c3, place_ref[2]) + ones_ref[...]
    aux_ref[...] = aux.astype(aux_ref.dtype)


def _fgate(x, w_fg, b_fg, batch, seq, *, tc=512):
    nt = seq // tc
    tri = jnp.asarray(np.tril(np.ones((tc, tc), np.float32)), _MXU)
    place = np.zeros((3, LANES, LANES), np.float32)
    for h in range(N_SELF_HEADS):
        for p in range(3):
            place[p, h, 3 * h + p] = 1.0
    ones = np.zeros((1, LANES), np.float32)
    ones[0, AUX_ONES] = 1.0
    return pl.pallas_call(
        _fgate_kernel,
        out_shape=jax.ShapeDtypeStruct((batch * seq, LANES), _MXU),
        grid=(batch, nt),
        in_specs=[pl.BlockSpec((tc, D_MODEL), lambda b, i: (b * nt + i, 0)),
                  _const_spec((D_MODEL, LANES)), _const_spec((1, LANES)),
                  _const_spec((tc, tc)), _const_spec((3, LANES, LANES)), _const_spec((1, LANES))],
        out_specs=pl.BlockSpec((tc, LANES), lambda b, i: (b * nt + i, 0)),
        scratch_shapes=[pltpu.VMEM((1, LANES), _F32)],
        compiler_params=_params("parallel", "arbitrary"),
        name="fgate",
    )(x, w_fg, b_fg, tri, jnp.asarray(place, _MXU), jnp.asarray(ones))


def _attn_kernel(*refs, tq, tk, nkv, causal, moba):
    if moba:
        qa_ref, ka_ref, va_ref, kbar_ref, o_ref = refs
    else:
        qa_ref, ka_ref, va_ref, o_ref = refs
    qi = pl.program_id(2)
    lane = lax.broadcasted_iota(jnp.int32, (tq, HEAD_BLOCK), 1)
    accs = []
    for hh in range(2):
        cs = slice(hh * HEAD_BLOCK, (hh + 1) * HEAD_BLOCK)
        q = qa_ref[0, :, cs]

        if moba:
            nb = kbar_ref.shape[1]
            gate = _dot_nt(q.astype(_F32), kbar_ref[0, :, cs])
            blk = lax.broadcasted_iota(jnp.int32, (tq, nb), 1)
            g = jnp.where(blk < qi, gate, -jnp.inf)
            sel = jnp.zeros((tq, nb), _F32)
            for _ in range(MOBA_TOPK):
                mx = jnp.max(g, axis=1, keepdims=True)
                idx = jnp.min(jnp.where(g == mx, blk, nb), axis=1, keepdims=True)
                hit = blk == idx
                sel = jnp.where(hit & (mx > -jnp.inf), 1.0, sel)
                g = jnp.where(hit, -jnp.inf, g)

        def scores(j):
            start = pl.multiple_of(j * tk, tk)
            k = ka_ref[0, pl.ds(start, tk), cs]
            v = va_ref[0, pl.ds(start, tk), cs]
            return _dot_nt(q, k), v

        first = qi if causal else 0
        s, v = scores(first)
        if causal:
            row = lax.broadcasted_iota(jnp.int32, (tq, tk), 0)
            col = lax.broadcasted_iota(jnp.int32, (tq, tk), 1)
            s = jnp.where(col <= row, s, MASK_VALUE)
        m0 = jnp.max(s, axis=1, keepdims=True)
        acc0 = _dot(jnp.exp(s - m0).astype(_MXU), v)

        def step(j, carry):
            m, acc = carry
            s, v = scores(j)
            if moba:
                picked = jnp.max(jnp.where(blk == j, sel, 0.0), axis=1, keepdims=True)
                s = jnp.where(picked > 0.0, s, MASK_VALUE)
            m_new = jnp.maximum(m, jnp.max(s, axis=1, keepdims=True))
            p = jnp.exp(s - m_new).astype(_MXU)
            return m_new, jnp.exp(m - m_new) * acc + _dot(p, v)

        if causal:
            _, acc = lax.fori_loop(0, qi, step, (m0, acc0))
        else:
            _, acc = lax.fori_loop(1, nkv, step, (m0, acc0))
        accs.append(acc)

    l_even = jnp.sum(jnp.where(lane == HEAD_DIM, accs[0], 0.0), axis=1, keepdims=True)
    l_odd = jnp.sum(jnp.where(lane == 0, accs[1], 0.0), axis=1, keepdims=True)
    out = jnp.where(lane < HEAD_DIM, accs[0] / l_even, accs[1] / l_odd)
    o_ref[0] = out.astype(o_ref.dtype)


def _attention(qa, ka, va, kbar, *, causal, moba, tq=256, tk=256):
    batch, seq, width = qa.shape
    seq_k = ka.shape[1]
    npairs = width // (2 * HEAD_BLOCK)
    if causal:
        assert tq == tk and seq == seq_k
    if moba:
        assert tq == MOBA_BLOCK and tk == MOBA_BLOCK
    pair = 2 * HEAD_BLOCK
    in_specs = [pl.BlockSpec((1, tq, pair), lambda b, p, i: (b, i, p)),
                pl.BlockSpec((1, seq_k, pair), lambda b, p, i: (b, 0, p)),
                pl.BlockSpec((1, seq_k, pair), lambda b, p, i: (b, 0, p))]
    args = [qa, ka, va]
    if moba:
        in_specs.append(pl.BlockSpec((1, kbar.shape[1], pair), lambda b, p, i: (b, 0, p)))
        args.append(kbar)
    return pl.pallas_call(
        functools.partial(_attn_kernel, tq=tq, tk=tk, nkv=seq_k // tk, causal=causal, moba=moba),
        out_shape=jax.ShapeDtypeStruct((batch, seq, npairs * HEAD_BLOCK), _MXU),
        grid=(batch, npairs, seq // tq),
        in_specs=in_specs,
        out_specs=pl.BlockSpec((1, tq, HEAD_BLOCK), lambda b, p, i: (b, i, p)),
        compiler_params=_params("parallel", "parallel", "arbitrary"),
        name="attn_moba" if moba else ("attn_fox" if causal else "attn_mem"),
    )(*args)


def _post_kernel(h_ref, os_ref, om_ref, wo1_ref, wo2_ref, g_ref, b_ref, rwh_ref, rwl_ref, rb_ref,
                 h1_ref, ti_ref, tg_ref, *, alpha):
    y = alpha * h_ref[...] + _dot(os_ref[...], wo1_ref[...]) + _dot(om_ref[...], wo2_ref[...])
    h1 = _layer_norm(y, g_ref[...], b_ref[...])
    h1_ref[...] = h1
    hi = h1.astype(_MXU)
    lo = (h1 - hi.astype(_F32)).astype(_MXU)
    rwh = rwh_ref[...]
    logits = _dot(hi, rwh) + _dot(hi, rwl_ref[...]) + _dot(lo, rwh) + rb_ref[...]
    lane = lax.broadcasted_iota(jnp.int32, logits.shape, 1)
    ti = jnp.zeros(logits.shape, jnp.int32)
    tg = jnp.zeros(logits.shape, _F32)
    top = None
    denom = None
    es = []
    for k in range(TOP_K):
        mx = jnp.max(logits, axis=1, keepdims=True)
        idx = jnp.min(jnp.where(logits == mx, lane, LANES), axis=1, keepdims=True)
        logits = jnp.where(lane == idx, -jnp.inf, logits)
        ti = jnp.where(lane == k, idx, ti)
        if k == 0:
            top = mx
        e = jnp.exp(mx - top)
        es.append(e)
        denom = e if denom is None else denom + e
    for k in range(TOP_K):
        tg = jnp.where(lane == k, es[k] / denom, tg)
    ti_ref[...] = ti
    tg_ref[...] = tg


def _post(h, o_self, o_mem, wo1, wo2, g, b, rwh, rwl, rb, alpha, *, tm=512):
    n = h.shape[0]
    row = lambda w: pl.BlockSpec((tm, w), lambda i: (i, 0))
    return pl.pallas_call(
        functools.partial(_post_kernel, alpha=alpha),
        out_shape=[jax.ShapeDtypeStruct((n, D_MODEL), _F32),
                   jax.ShapeDtypeStruct((n, LANES), jnp.int32),
                   jax.ShapeDtypeStruct((n, LANES), _F32)],
        grid=(n // tm,),
        in_specs=[row(D_MODEL), row(SELF_WIDTH), row(MEM_WIDTH),
                  _const_spec(wo1.shape), _const_spec(wo2.shape),
                  _const_spec((1, D_MODEL)), _const_spec((1, D_MODEL)),
                  _const_spec(rwh.shape), _const_spec(rwl.shape), _const_spec((1, LANES))],
        out_specs=[row(D_MODEL), row(LANES), row(LANES)],
        compiler_params=_params("parallel"),
        name="post_attn",
    )(h, o_self, o_mem, wo1, wo2, g, b, rwh, rwl, rb)


def _gather_rows(idx_vmem_ref, idx_smem, src_hbm, dst, sem, n_rows):
    to_smem = pltpu.make_async_copy(idx_vmem_ref, idx_smem, sem.at[0])
    to_smem.start()
    to_smem.wait()

    def issue(r, carry):
        t = idx_smem[0, r]
        pltpu.make_async_copy(src_hbm.at[pl.ds(t, 1), :], dst.at[pl.ds(r, 1), :], sem.at[1]).start()
        return carry

    lax.fori_loop(0, n_rows, issue, 0)
    pltpu.make_async_copy(src_hbm.at[pl.ds(0, n_rows), :], dst, sem.at[1]).wait()


def _moe_kernel(be_ref, nv_ref, tok_ref, wt_ref, h1_hbm, wgu_ref, bgu_ref, wdn_ref, bdn_ref,
                y_ref, xbuf, idx_smem, sem, *, bm):
    i = pl.program_id(0)

    @pl.when(i < nv_ref[0])
    def _():
        _gather_rows(tok_ref.at[0], idx_smem, h1_hbm, xbuf, sem, bm)
        x = xbuf[...].astype(_MXU)
        gu = _dot(x, wgu_ref[0]) + bgu_ref[0]
        g = jnp.minimum(gu[:, :D_FF], SWIGLU_LIMIT)
        u = jnp.clip(gu[:, D_FF:], -SWIGLU_LIMIT, SWIGLU_LIMIT)
        a = g * jax.nn.sigmoid(SWIGLU_ALPHA * g) * (u + 1.0)
        y = _dot(a.astype(_MXU), wdn_ref[0]) + bdn_ref[0]
        y_ref[...] = y * wt_ref[...]

    @pl.when(i >= nv_ref[0])
    def _():
        y_ref[...] = jnp.zeros_like(y_ref)


def _moe_experts(blk_e, nvalid, row_tok, row_w, h1, wgu, bgu, wdn, bdn, *, bm):
    nblk = row_tok.shape[0]
    grid_spec = pltpu.PrefetchScalarGridSpec(
        num_scalar_prefetch=2,
        grid=(nblk,),
        in_specs=[pl.BlockSpec((1, 1, bm), lambda i, be, nv: (i, 0, 0)),
                  pl.BlockSpec((bm, 1), lambda i, be, nv: (i, 0)),
                  pl.BlockSpec(memory_space=pl.ANY),
                  pl.BlockSpec((1, D_MODEL, 2 * D_FF), lambda i, be, nv: (be[i], 0, 0)),
                  pl.BlockSpec((1, 1, 2 * D_FF), lambda i, be, nv: (be[i], 0, 0)),
                  pl.BlockSpec((1, D_FF, D_MODEL), lambda i, be, nv: (be[i], 0, 0)),
                  pl.BlockSpec((1, 1, D_MODEL), lambda i, be, nv: (be[i], 0, 0))],
        out_specs=pl.BlockSpec((bm, D_MODEL), lambda i, be, nv: (i, 0)),
        scratch_shapes=[pltpu.VMEM((bm, D_MODEL), _F32),
                        pltpu.SMEM((1, bm), jnp.int32),
                        pltpu.SemaphoreType.DMA((2,))],
    )
    return pl.pallas_call(
        functools.partial(_moe_kernel, bm=bm),
        out_shape=jax.ShapeDtypeStruct((nblk * bm, D_MODEL), _F32),
        grid_spec=grid_spec,
        compiler_params=_params("arbitrary"),
        name="moe_experts",
    )(blk_e, nvalid, row_tok, row_w, h1, wgu, bgu, wdn, bdn)


def _combine_kernel(pos_ref, h1_ref, y_hbm, g_ref, b_ref, o_ref, ybuf, idx_smem, sem, *, tm, alpha):
    _gather_rows(pos_ref.at[0], idx_smem, y_hbm, ybuf, sem, TOP_K * tm)
    f = ybuf[0:tm, :]
    for k in range(1, TOP_K):
        f = f + ybuf[k * tm:(k + 1) * tm, :]
    o_ref[...] = _layer_norm(alpha * h1_ref[...] + f, g_ref[...], b_ref[...])


def _combine(pos_t, h1, y_sorted, g, b, alpha, *, tm):
    n = h1.shape[0]
    return pl.pallas_call(
        functools.partial(_combine_kernel, tm=tm, alpha=alpha),
        out_shape=jax.ShapeDtypeStruct((n, D_MODEL), _F32),
        grid=(n // tm,),
        in_specs=[pl.BlockSpec((1, 1, TOP_K * tm), lambda i: (i, 0, 0)),
                  pl.BlockSpec((tm, D_MODEL), lambda i: (i, 0)),
                  pl.BlockSpec(memory_space=pl.ANY),
                  _const_spec((1, D_MODEL)), _const_spec((1, D_MODEL))],
        out_specs=pl.BlockSpec((tm, D_MODEL), lambda i: (i, 0)),
        scratch_shapes=[pltpu.VMEM((TOP_K * tm, D_MODEL), _F32),
                        pltpu.SMEM((1, TOP_K * tm), jnp.int32),
                        pltpu.SemaphoreType.DMA((2,))],
        compiler_params=_params("arbitrary"),
        name="moe_combine",
    )(pos_t, h1, y_sorted, g, b)


def _alibi_slopes(n):
    def pow2(m):
        start = 2.0 ** (-(2.0 ** -(math.log2(m) - 3)))
        return [start * start ** i for i in range(m)]
    if math.log2(n).is_integer():
        s = pow2(n)
    else:
        c = 2 ** math.floor(math.log2(n))
        s = pow2(c) + pow2(2 * c)[0::2][: n - c]
    return np.array(s, dtype=np.float32)


def _np_split3(x):
    as_mxu = lambda a: np.asarray(a, np.float32).astype(jnp.bfloat16).astype(np.float32)
    p1 = as_mxu(x)
    p2 = as_mxu(x - p1)
    p3 = as_mxu(x - p1 - p2)
    return p1, p2, p3


def _pad_heads(w, n_heads, scale=None):
    k = w.shape[0]
    w = w.reshape(k, n_heads, HEAD_DIM)
    if scale is not None:
        w = w * scale
    return jnp.pad(w, ((0, 0), (0, 0), (0, HEAD_BLOCK - HEAD_DIM))).reshape(k, n_heads * HEAD_BLOCK)


def _pad_heads_alternating(w, n_heads):
    k = w.shape[0]
    w = w.reshape(k, n_heads // 2, 2, HEAD_DIM)
    z = jnp.zeros_like(w[:, :, 0])
    even = jnp.concatenate([w[:, :, 0], z], axis=-1)
    odd = jnp.concatenate([z, w[:, :, 1]], axis=-1)
    return jnp.stack([even, odd], axis=2).reshape(k, n_heads * HEAD_BLOCK)


def _value_ones(n_heads):
    s = np.zeros((LANES, n_heads * HEAD_BLOCK), np.float32)
    for h in range(n_heads):
        s[AUX_ONES, h * HEAD_BLOCK + (HEAD_DIM if h % 2 == 0 else 0)] = 1.0
    return s


def _fox_bias_placement():
    sq = np.zeros((LANES, SELF_AUG), np.float32)
    sk = np.zeros((LANES, SELF_AUG), np.float32)
    for h in range(N_SELF_HEADS):
        base = h * HEAD_BLOCK + HEAD_DIM
        for p in range(3):
            sq[3 * h + p, base + p] = 1.0
            sk[AUX_ONES, base + p] = 1.0
            sq[AUX_ONES, base + 3 + p] = 1.0
            sk[3 * h + p, base + 3 + p] = -1.0
    return sq, sk


def _moba_bias_placement():
    slopes = _np_split3(_alibi_slopes(N_SELF_HEADS))
    sq = np.zeros((LANES, SELF_AUG), np.float32)
    sk = np.zeros((LANES, SELF_AUG), np.float32)
    for h in range(N_SELF_HEADS):
        base = h * HEAD_BLOCK + HEAD_DIM
        for tpart in range(2):
            for p in range(3):
                c = base + 3 * tpart + p
                sq[tpart, c] = 1.0
                sk[AUX_ONES, c] = -slopes[p][h]
                sq[AUX_ONES, c + 6] = slopes[p][h]
                sk[tpart, c + 6] = 1.0
    return sq, sk


def _moba_aux(batch, seq):
    t = np.arange(seq)
    aux = np.zeros((seq, LANES), np.float32)
    aux[:, 0] = (t // 64) * 64
    aux[:, 1] = t % 64
    aux[:, AUX_ONES] = 1.0
    return jnp.asarray(np.tile(aux, (batch, 1)), _MXU)


def _route(top_i, top_g, bm, nblk):
    n = top_i.shape[0]
    a = n * TOP_K
    flat_e = top_i.reshape(a)
    onehot = (flat_e[:, None] == jnp.arange(N_EXPERTS, dtype=jnp.int32)[None, :]).astype(jnp.int32)
    csum = jnp.cumsum(onehot, axis=0)
    rank = jnp.take_along_axis(csum, flat_e[:, None], axis=1)[:, 0] - 1
    counts = csum[-1]
    gsz = ((counts + bm - 1) // bm) * bm
    gend = jnp.cumsum(gsz)
    pos = (gend - gsz)[flat_e] + rank
    row_tok = jnp.zeros((nblk * bm,), jnp.int32).at[pos].set(jnp.arange(a, dtype=jnp.int32) // TOP_K)
    row_w = jnp.zeros((nblk * bm,), _F32).at[pos].set(top_g.reshape(a))
    blk_e = jnp.searchsorted(gend, jnp.arange(nblk, dtype=jnp.int32) * bm, side='right')
    blk_e = jnp.minimum(blk_e, N_EXPERTS - 1).astype(jnp.int32)
    nvalid = (gend[-1:] // bm).astype(jnp.int32)
    return pos.reshape(n, TOP_K), row_tok, row_w, blk_e, nvalid


def kernel(x, mem, w_in_moba, w_in_fox, b_fgate, w_mem_kv, w_o, ln1_g, ln1_b, router_w, router_b,
           w_gate_up, b_gate_up, w_down, b_down, ln2_g, ln2_b):
    batch, seq, d = x.shape
    depth = w_o.shape[0]
    n = batch * seq
    n_mem = mem.shape[1]
    alpha = (2 * depth) ** 0.25
    bm = 512
    nblk = (n * TOP_K) // bm + N_EXPERTS
    tm_c = 128

    ones_aux = np.zeros((batch * n_mem, LANES), np.float32)
    ones_aux[:, AUX_ONES] = 1.0
    mk, mv = _proj(mem.reshape(batch * n_mem, d), jnp.asarray(ones_aux, _MXU),
                   [_pad_heads(w_mem_kv[:, :MEM_WIDTH], N_MEM_HEADS).astype(_MXU),
                    _pad_heads_alternating(w_mem_kv[:, MEM_WIDTH:], N_MEM_HEADS).astype(_MXU)],
                   [None, jnp.asarray(_value_ones(N_MEM_HEADS), _MXU)])
    mk = mk.reshape(batch, n_mem, MEM_AUG)
    mv = mv.reshape(batch, n_mem, MEM_AUG)

    sv = jnp.asarray(_value_ones(N_SELF_HEADS), _MXU)
    moba_aux = _moba_aux(batch, seq)
    moba_s = [jnp.asarray(s, _MXU) for s in _moba_bias_placement()]
    fox_s = [jnp.asarray(s, _MXU) for s in _fox_bias_placement()]
    pad_lanes = lambda w: jnp.pad(w, ((0, 0), (0, LANES - w.shape[1])))

    h = x.reshape(n, d)
    for i in range(depth):
        j = i // 2
        is_moba = i % 2 == 0
        w_in = w_in_moba[j] if is_moba else w_in_fox[j]
        ws = [_pad_heads(w_in[:, :SELF_WIDTH], N_SELF_HEADS, QK_SCALE).astype(_MXU),
              _pad_heads(w_in[:, SELF_WIDTH:2 * SELF_WIDTH], N_SELF_HEADS).astype(_MXU),
              _pad_heads_alternating(w_in[:, 2 * SELF_WIDTH:3 * SELF_WIDTH], N_SELF_HEADS).astype(_MXU),
              _pad_heads(w_in[:, 3 * SELF_WIDTH:IN_COLS], N_MEM_HEADS, QK_SCALE).astype(_MXU)]
        if is_moba:
            aux = moba_aux
            sq, sk = moba_s
        else:
            aux = _fgate(h, pad_lanes(w_in[:, IN_COLS:]).astype(_MXU), pad_lanes(b_fgate[j][None, :]),
                         batch, seq)
            sq, sk = fox_s
        outs = _proj(h, aux, ws, [sq, sk, sv, None], with_kbar=is_moba)
        qa, ka, va, qm = (o.reshape(batch, seq, -1) for o in outs[:4])
        kbar = outs[4].reshape(batch, seq // MOBA_BLOCK, SELF_AUG) if is_moba else None
        o_self = _attention(qa, ka, va, kbar, causal=True, moba=is_moba)
        o_mem = _attention(qm, mk, mv, None, causal=False, moba=False)

        rw = pad_lanes(router_w[i])
        rwh = rw.astype(_MXU)
        rwl = (rw - rwh.astype(_F32)).astype(_MXU)
        rb = jnp.pad(router_b[i][None, :], ((0, 0), (0, LANES - N_EXPERTS)), constant_values=MASK_VALUE)
        h1, ti, tg = _post(h, o_self.reshape(n, SELF_WIDTH), o_mem.reshape(n, MEM_WIDTH),
                           w_o[i][:SELF_WIDTH].astype(_MXU), w_o[i][SELF_WIDTH:].astype(_MXU),
                           ln1_g[i][None, :], ln1_b[i][None, :], rwh, rwl, rb, alpha)

        pos, row_tok, row_w, blk_e, nvalid = _route(ti[:, :TOP_K], tg[:, :TOP_K], bm, nblk)
        y_sorted = _moe_experts(blk_e, nvalid, row_tok.reshape(nblk, 1, bm), row_w.reshape(nblk * bm, 1), h1,
                                w_gate_up[i].astype(_MXU), b_gate_up[i][:, None, :],
                                w_down[i].astype(_MXU), b_down[i][:, None, :], bm=bm)
        pos_t = pos.reshape(n // tm_c, tm_c, TOP_K).transpose(0, 2, 1).reshape(n // tm_c, 1, TOP_K * tm_c)
        h = _combine(pos_t, h1, y_sorted, ln2_g[i][None, :], ln2_b[i][None, :], alpha, tm=tm_c)
    return h.reshape(batch, seq, d)
```

```python
import functools
import math

import numpy as np
import jax
import jax.numpy as jnp
from jax import lax
from jax.experimental import pallas as pl
from jax.experimental.pallas import tpu as pltpu

D_MODEL = 1024
HEAD_DIM = 64
N_SELF_HEADS = 12
N_MEM_HEADS = 4
SELF_WIDTH = N_SELF_HEADS * HEAD_DIM
MEM_WIDTH = N_MEM_HEADS * HEAD_DIM
IN_COLS = 3 * SELF_WIDTH + MEM_WIDTH
MOBA_BLOCK = 256
MOBA_TOPK = 3
N_EXPERTS = 32
TOP_K = 4
D_FF = D_MODEL
SWIGLU_LIMIT = 7.0
SWIGLU_ALPHA = 1.702
LN_EPS = 1e-5

LANES = 128
HEAD_BLOCK = LANES
SELF_AUG = N_SELF_HEADS * HEAD_BLOCK
MEM_AUG = N_MEM_HEADS * HEAD_BLOCK
AUX_ONES = 36
QK_SCALE = HEAD_DIM ** -0.5
MASK_VALUE = -1e30
VMEM_LIMIT = 56 * 1024 * 1024

_MXU = jnp.bfloat16
_F32 = jnp.float32


def _dot(a, b):
    return jnp.dot(a, b, preferred_element_type=_F32)


def _dot_nt(a, b):
    return lax.dot_general(a, b, (((1,), (1,)), ((), ())), preferred_element_type=_F32)


def _split3(x):
    p1 = x.astype(_MXU)
    r1 = x - p1.astype(_F32)
    p2 = r1.astype(_MXU)
    r2 = r1 - p2.astype(_F32)
    return p1, p2, r2.astype(_MXU)


def _layer_norm(y, g, b):
    mu = jnp.mean(y, axis=-1, keepdims=True)
    yc = y - mu
    var = jnp.mean(yc * yc, axis=-1, keepdims=True)
    return yc * lax.rsqrt(var + LN_EPS) * g + b


def _params(*sem):
    return pltpu.CompilerParams(dimension_semantics=sem, vmem_limit_bytes=VMEM_LIMIT)


def _const_spec(shape):
    nd = len(shape)
    return pl.BlockSpec(shape, lambda *_: (0,) * nd)


def _proj_kernel(*refs, n_out, has_aux, with_kbar):
    x_ref, aux_ref = refs[0], refs[1]
    w_refs = refs[2:2 + n_out]
    s_refs = refs[2 + n_out:2 + 2 * n_out]
    o_refs = refs[2 + 2 * n_out:2 + 3 * n_out]
    x = x_ref[...].astype(_MXU)
    aux = aux_ref[...]
    for i in range(n_out):
        acc = _dot(x, w_refs[i][...])
        if has_aux[i]:
            acc = acc + _dot(aux, s_refs[i][...])
        o_refs[i][...] = acc.astype(o_refs[i].dtype)
        if with_kbar and i == 1:
            kbar_ref = refs[2 + 3 * n_out]
            lane = lax.broadcasted_iota(jnp.int32, (1, acc.shape[1]), 1)
            kbar = jnp.mean(acc, axis=0, keepdims=True)
            kbar_ref[0] = jnp.where(lane % HEAD_BLOCK < HEAD_DIM, kbar, 0.0)


def _proj(x, aux, ws, ss, *, with_kbar=False, tm=256):
    m, kdim = x.shape
    n_out = len(ws)
    has_aux = tuple(s is not None for s in ss)
    ss = [s if s is not None else jnp.zeros((LANES, LANES), _MXU) for s in ss]
    out_shape = [jax.ShapeDtypeStruct((m, w.shape[1]), _MXU) for w in ws]
    out_specs = [pl.BlockSpec((tm, w.shape[1]), lambda i: (i, 0)) for w in ws]
    if with_kbar:
        assert tm == MOBA_BLOCK
        out_shape.append(jax.ShapeDtypeStruct((m // tm, 1, ws[1].shape[1]), _F32))
        out_specs.append(pl.BlockSpec((1, 1, ws[1].shape[1]), lambda i: (i, 0, 0)))
    in_specs = [pl.BlockSpec((tm, kdim), lambda i: (i, 0)),
                pl.BlockSpec((tm, LANES), lambda i: (i, 0))]
    in_specs += [_const_spec(w.shape) for w in ws]
    in_specs += [_const_spec(s.shape) for s in ss]
    return pl.pallas_call(
        functools.partial(_proj_kernel, n_out=n_out, has_aux=has_aux, with_kbar=with_kbar),
        out_shape=out_shape,
        grid=(m // tm,),
        in_specs=in_specs,
        out_specs=out_specs,
        compiler_params=_params("parallel"),
        name="proj",
    )(x, aux, *ws, *ss)


def _fgate_kernel(x_ref, w_ref, b_ref, tri_ref, place_ref, ones_ref, aux_ref, carry_ref):
    @pl.when(pl.program_id(1) == 0)
    def _():
        carry_ref[...] = jnp.zeros_like(carry_ref)

    z = _dot(x_ref[...].astype(_MXU), w_ref[...]) + b_ref[...]
    ls = jnp.minimum(z, 0.0) - jnp.log(1.0 + jnp.exp(-jnp.abs(z)))
    tri = tri_ref[...]
    l1, l2, l3 = _split3(ls)
    c = _dot(tri, l1) + _dot(tri, l2) + _dot(tri, l3) + carry_ref[...]
    carry_ref[...] = c[c.shape[0] - 1:c.shape[0], :]
    c1, c2, c3 = _split3(c)
    aux = _dot(c1, place_ref[0]) + _dot(c2, place_ref[1]) + _dot(c3, place_ref[2]) + ones_ref[...]
    aux_ref[...] = aux.astype(aux_ref.dtype)


def _fgate(x, w_fg, b_fg, batch, seq, *, tc=512):
    nt = seq // tc
    tri = jnp.asarray(np.tril(np.ones((tc, tc), np.float32)), _MXU)
    place = np.zeros((3, LANES, LANES), np.float32)
    for h in range(N_SELF_HEADS):
        for p in range(3):
            place[p, h, 3 * h + p] = 1.0
    ones = np.zeros((1, LANES), np.float32)
    ones[0, AUX_ONES] = 1.0
    return pl.pallas_call(
        _fgate_kernel,
        out_shape=jax.ShapeDtypeStruct((batch * seq, LANES), _MXU),
        grid=(batch, nt),
        in_specs=[pl.BlockSpec((tc, D_MODEL), lambda b, i: (b * nt + i, 0)),
                  _const_spec((D_MODEL, LANES)), _const_spec((1, LANES)),
                  _const_spec((tc, tc)), _const_spec((3, LANES, LANES)), _const_spec((1, LANES))],
        out_specs=pl.BlockSpec((tc, LANES), lambda b, i: (b * nt + i, 0)),
        scratch_shapes=[pltpu.VMEM((1, LANES), _F32)],
        compiler_params=_params("parallel", "arbitrary"),
        name="fgate",
    )(x, w_fg, b_fg, tri, jnp.asarray(place, _MXU), jnp.asarray(ones))


def _attn_kernel(*refs, tq, tkc, n_chunks, causal, moba):
    if moba:
        qa_ref, ka_ref, va_ref, kbar_ref, o_ref = refs
    else:
        qa_ref, ka_ref, va_ref, o_ref = refs
    qi = pl.program_id(2)
    lane = lax.broadcasted_iota(jnp.int32, (tq, HEAD_BLOCK), 1)
    col = lax.broadcasted_iota(jnp.int32, (tq, tkc), 1)
    blocks_per_chunk = tkc // MOBA_BLOCK
    own_blk = (qi * tq + lax.broadcasted_iota(jnp.int32, (tq, 1), 0)) // MOBA_BLOCK

    heads = []
    for hh in range(2):
        cs = slice(hh * HEAD_BLOCK, (hh + 1) * HEAD_BLOCK)
        q = qa_ref[0, :, cs]
        sel = None
        if moba:
            nb = kbar_ref.shape[1]
            gate = _dot_nt(q.astype(_F32), kbar_ref[0, :, cs])
            blk = lax.broadcasted_iota(jnp.int32, (tq, nb), 1)
            g = jnp.where(blk < own_blk, gate, -jnp.inf)
            sel = jnp.zeros((tq, nb), _F32)
            for _ in range(MOBA_TOPK):
                mx = jnp.max(g, axis=1, keepdims=True)
                idx = jnp.min(jnp.where(g == mx, blk, nb), axis=1, keepdims=True)
                hit = blk == idx
                sel = jnp.where(hit & (mx > -jnp.inf), 1.0, sel)
                g = jnp.where(hit, -jnp.inf, g)
        heads.append((cs, q, sel))

    def scores(head, c):
        cs, q, _ = head
        start = pl.multiple_of(c * tkc, tkc)
        return _dot_nt(q, ka_ref[0, pl.ds(start, tkc), cs]), va_ref[0, pl.ds(start, tkc), cs]

    def moba_mask(s, sel, c, own_too):
        blk = lax.broadcasted_iota(jnp.int32, sel.shape, 1)
        parts = []
        for u in range(blocks_per_chunk):
            jb = c * blocks_per_chunk + u
            ok = jnp.max(jnp.where(blk == jb, sel, 0.0), axis=1, keepdims=True)
            if own_too:
                ok = jnp.where(jb == own_blk, 1.0, ok)
            parts.append(jnp.where(ok > 0.0, s[:, u * MOBA_BLOCK:(u + 1) * MOBA_BLOCK], MASK_VALUE))
        return parts[0] if len(parts) == 1 else jnp.concatenate(parts, axis=1)

    first = (qi * tq) // tkc if causal else 0
    carry = []
    for head in heads:
        s, v = scores(head, first)
        if causal:
            row = lax.broadcasted_iota(jnp.int32, (tq, tkc), 0)
            s = jnp.where(first * tkc + col <= qi * tq + row, s, MASK_VALUE)
            if moba:
                s = moba_mask(s, head[2], first, True)
        m0 = jnp.max(s, axis=1, keepdims=True)
        carry += [m0, _dot(jnp.exp(s - m0).astype(_MXU), v)]

    def step(c, carry):
        out = []
        for hh, head in enumerate(heads):
            m, acc = carry[2 * hh], carry[2 * hh + 1]
            s, v = scores(head, c)
            if moba:
                s = moba_mask(s, head[2], c, False)
            m_new = jnp.maximum(m, jnp.max(s, axis=1, keepdims=True))
            p = jnp.exp(s - m_new).astype(_MXU)
            out += [m_new, jnp.exp(m - m_new) * acc + _dot(p, v)]
        return tuple(out)

    if causal:
        carry = lax.fori_loop(0, first, step, tuple(carry))
    else:
        carry = lax.fori_loop(1, n_chunks, step, tuple(carry))
    acc_even, acc_odd = carry[1], carry[3]

    l_even = jnp.sum(jnp.where(lane == HEAD_DIM, acc_even, 0.0), axis=1, keepdims=True)
    l_odd = jnp.sum(jnp.where(lane == 0, acc_odd, 0.0), axis=1, keepdims=True)
    out = jnp.where(lane < HEAD_DIM, acc_even / l_even, acc_odd / l_odd)
    o_ref[0] = out.astype(o_ref.dtype)


def _attention(qa, ka, va, kbar, *, causal, moba, tq=256, tkc=1024):
    batch, seq, width = qa.shape
    seq_k = ka.shape[1]
    tkc = min(tkc, seq_k)
    npairs = width // (2 * HEAD_BLOCK)
    assert seq % tq == 0 and seq_k % tkc == 0 and tkc % MOBA_BLOCK == 0
    if causal:
        assert seq == seq_k and tkc % tq == 0
    if moba:
        assert tq % MOBA_BLOCK == 0
    pair = 2 * HEAD_BLOCK
    in_specs = [pl.BlockSpec((1, tq, pair), lambda b, p, i: (b, i, p)),
                pl.BlockSpec((1, seq_k, pair), lambda b, p, i: (b, 0, p)),
                pl.BlockSpec((1, seq_k, pair), lambda b, p, i: (b, 0, p))]
    args = [qa, ka, va]
    if moba:
        in_specs.append(pl.BlockSpec((1, kbar.shape[1], pair), lambda b, p, i: (b, 0, p)))
        args.append(kbar)
    return pl.pallas_call(
        functools.partial(_attn_kernel, tq=tq, tkc=tkc, n_chunks=seq_k // tkc, causal=causal, moba=moba),
        out_shape=jax.ShapeDtypeStruct((batch, seq, npairs * HEAD_BLOCK), _MXU),
        grid=(batch, npairs, seq // tq),
        in_specs=in_specs,
        out_specs=pl.BlockSpec((1, tq, HEAD_BLOCK), lambda b, p, i: (b, i, p)),
        compiler_params=_params("parallel", "parallel", "arbitrary"),
        name="attn_moba" if moba else ("attn_fox" if causal else "attn_mem"),
    )(*args)


def _post_kernel(h_ref, os_ref, om_ref, wo1_ref, wo2_ref, g_ref, b_ref, rwh_ref, rwl_ref, rb_ref, tri_ref,
                 h1_ref, ti_ref, tg_ref, rk_ref, cnt_ref, carry_ref, *, alpha):
    @pl.when(pl.program_id(0) == 0)
    def _():
        carry_ref[...] = jnp.zeros_like(carry_ref)

    y = alpha * h_ref[...] + _dot(os_ref[...], wo1_ref[...]) + _dot(om_ref[...], wo2_ref[...])
    h1 = _layer_norm(y, g_ref[...], b_ref[...])
    h1_ref[...] = h1
    hi = h1.astype(_MXU)
    lo = (h1 - hi.astype(_F32)).astype(_MXU)
    rwh = rwh_ref[...]
    logits = _dot(hi, rwh) + _dot(hi, rwl_ref[...]) + _dot(lo, rwh) + rb_ref[...]
    lane = lax.broadcasted_iota(jnp.int32, logits.shape, 1)
    ti = jnp.zeros(logits.shape, jnp.int32)
    tg = jnp.zeros(logits.shape, _F32)
    chosen = jnp.zeros(logits.shape, _F32)
    top = None
    denom = None
    es, hits = [], []
    for k in range(TOP_K):
        mx = jnp.max(logits, axis=1, keepdims=True)
        idx = jnp.min(jnp.where(logits == mx, lane, LANES), axis=1, keepdims=True)
        hit = lane == idx
        hits.append(hit)
        logits = jnp.where(hit, -jnp.inf, logits)
        chosen = jnp.where(hit, 1.0, chosen)
        ti = jnp.where(lane == k, idx, ti)
        if k == 0:
            top = mx
        e = jnp.exp(mx - top)
        es.append(e)
        denom = e if denom is None else denom + e
    earlier = _dot(tri_ref[...], chosen.astype(_MXU)) + carry_ref[...]
    carry_ref[...] = carry_ref[...] + jnp.sum(chosen, axis=0, keepdims=True)
    rk = jnp.zeros(logits.shape, _F32)
    for k in range(TOP_K):
        tg = jnp.where(lane == k, es[k] / denom, tg)
        rank_k = jnp.sum(jnp.where(hits[k], earlier, 0.0), axis=1, keepdims=True)
        rk = jnp.where(lane == k, rank_k, rk)
    ti_ref[...] = ti
    tg_ref[...] = tg
    rk_ref[...] = rk.astype(jnp.int32)
    cnt_ref[0] = carry_ref[...].astype(jnp.int32)


def _post(h, o_self, o_mem, wo1, wo2, g, b, rwh, rwl, rb, alpha, *, tm=512):
    n = h.shape[0]
    row = lambda w: pl.BlockSpec((tm, w), lambda i: (i, 0))
    tri = jnp.asarray(np.tril(np.ones((tm, tm), np.float32), -1), _MXU)
    return pl.pallas_call(
        functools.partial(_post_kernel, alpha=alpha),
        out_shape=[jax.ShapeDtypeStruct((n, D_MODEL), _F32),
                   jax.ShapeDtypeStruct((n, LANES), jnp.int32),
                   jax.ShapeDtypeStruct((n, LANES), _F32),
                   jax.ShapeDtypeStruct((n, LANES), jnp.int32),
                   jax.ShapeDtypeStruct((n // tm, 1, LANES), jnp.int32)],
        grid=(n // tm,),
        in_specs=[row(D_MODEL), row(SELF_WIDTH), row(MEM_WIDTH),
                  _const_spec(wo1.shape), _const_spec(wo2.shape),
                  _const_spec((1, D_MODEL)), _const_spec((1, D_MODEL)),
                  _const_spec(rwh.shape), _const_spec(rwl.shape), _const_spec((1, LANES)),
                  _const_spec((tm, tm))],
        out_specs=[row(D_MODEL), row(LANES), row(LANES), row(LANES),
                   pl.BlockSpec((1, 1, LANES), lambda i: (i, 0, 0))],
        scratch_shapes=[pltpu.VMEM((1, LANES), _F32)],
        compiler_params=_params("arbitrary"),
        name="post_attn",
    )(h, o_self, o_mem, wo1, wo2, g, b, rwh, rwl, rb, tri)


def _load_indices(idx_vmem_ref, idx_smem, sem):
    to_smem = pltpu.make_async_copy(idx_vmem_ref, idx_smem, sem)
    to_smem.start()
    to_smem.wait()


def _dispatch_kernel(pos_ref, h1_ref, xs_in_hbm, xs_hbm, idx_smem, sem, *, tm):
    del xs_in_hbm
    _load_indices(pos_ref.at[0], idx_smem, sem.at[0])
    for r in range(TOP_K * tm):
        p = idx_smem[0, r]
        pltpu.make_async_copy(h1_ref.at[pl.ds(r % tm, 1), :], xs_hbm.at[pl.ds(p, 1), :], sem.at[1]).start()
    for k in range(TOP_K):
        pltpu.make_async_copy(h1_ref, xs_hbm.at[pl.ds(0, tm), :], sem.at[1]).wait()


def _dispatch(pos_t, h1, n_rows, *, tm):
    n = h1.shape[0]
    return pl.pallas_call(
        functools.partial(_dispatch_kernel, tm=tm),
        out_shape=jax.ShapeDtypeStruct((n_rows, D_MODEL), _F32),
        grid=(n // tm,),
        in_specs=[pl.BlockSpec((1, 1, TOP_K * tm), lambda i: (i, 0, 0)),
                  pl.BlockSpec((tm, D_MODEL), lambda i: (i, 0)),
                  pl.BlockSpec(memory_space=pl.ANY)],
        out_specs=pl.BlockSpec(memory_space=pl.ANY),
        scratch_shapes=[pltpu.SMEM((1, TOP_K * tm), jnp.int32),
                        pltpu.SemaphoreType.DMA((2,))],
        input_output_aliases={2: 0},
        compiler_params=_params("arbitrary"),
        name="moe_dispatch",
    )(pos_t, h1, jnp.zeros((n_rows, D_MODEL), _F32))


def _combine_kernel(pos_ref, h1_ref, tg_ref, y_hbm, g_ref, b_ref, o_ref, ybuf, idx_smem, sem, *, tm, alpha):
    _load_indices(pos_ref.at[0], idx_smem, sem.at[0])

    for r in range(TOP_K * tm):
        p = idx_smem[0, r]
        pltpu.make_async_copy(y_hbm.at[pl.ds(p, 1), :], ybuf.at[pl.ds(r, 1), :], sem.at[1]).start()
    pltpu.make_async_copy(y_hbm.at[pl.ds(0, TOP_K * tm), :], ybuf, sem.at[1]).wait()
    tg = tg_ref[...]
    f = alpha * h1_ref[...]
    for k in range(TOP_K):
        f = f + tg[:, k:k + 1] * ybuf[k * tm:(k + 1) * tm, :]
    o_ref[...] = _layer_norm(f, g_ref[...], b_ref[...])


def _combine(pos_t, h1, tg, y_sorted, g, b, alpha, *, tm):
    n = h1.shape[0]
    return pl.pallas_call(
        functools.partial(_combine_kernel, tm=tm, alpha=alpha),
        out_shape=jax.ShapeDtypeStruct((n, D_MODEL), _F32),
        grid=(n // tm,),
        in_specs=[pl.BlockSpec((1, 1, TOP_K * tm), lambda i: (i, 0, 0)),
                  pl.BlockSpec((tm, D_MODEL), lambda i: (i, 0)),
                  pl.BlockSpec((tm, LANES), lambda i: (i, 0)),
                  pl.BlockSpec(memory_space=pl.ANY),
                  _const_spec((1, D_MODEL)), _const_spec((1, D_MODEL))],
        out_specs=pl.BlockSpec((tm, D_MODEL), lambda i: (i, 0)),
        scratch_shapes=[pltpu.VMEM((TOP_K * tm, D_MODEL), _F32),
                        pltpu.SMEM((1, TOP_K * tm), jnp.int32),
                        pltpu.SemaphoreType.DMA((2,))],
        compiler_params=_params("arbitrary"),
        name="moe_combine",
    )(pos_t, h1, tg, y_sorted, g, b)


def _moe_kernel(be_ref, nv_ref, x_ref, wgu_ref, bgu_ref, wdn_ref, bdn_ref, y_ref, wgu_mxu, wdn_mxu):
    i = pl.program_id(0)
    valid = i < nv_ref[0]
    new_expert = jnp.logical_or(i == 0, be_ref[i] != be_ref[jnp.maximum(i - 1, 0)])

    @pl.when(jnp.logical_and(valid, new_expert))
    def _():
        wgu_mxu[...] = wgu_ref[0].astype(_MXU)
        wdn_mxu[...] = wdn_ref[0].astype(_MXU)

    @pl.when(valid)
    def _():
        gu = _dot(x_ref[...].astype(_MXU), wgu_mxu[...]) + bgu_ref[0]
        g = jnp.minimum(gu[:, :D_FF], SWIGLU_LIMIT)
        u = jnp.clip(gu[:, D_FF:], -SWIGLU_LIMIT, SWIGLU_LIMIT)
        a = g * jax.nn.sigmoid(SWIGLU_ALPHA * g) * (u + 1.0)
        y_ref[...] = _dot(a.astype(_MXU), wdn_mxu[...]) + bdn_ref[0]

    @pl.when(jnp.logical_not(valid))
    def _():
        y_ref[...] = jnp.zeros_like(y_ref)


def _moe_experts(blk_e, nvalid, x_sorted, wgu, bgu, wdn, bdn, *, bm):
    nblk = x_sorted.shape[0] // bm
    grid_spec = pltpu.PrefetchScalarGridSpec(
        num_scalar_prefetch=2,
        grid=(nblk,),
        in_specs=[pl.BlockSpec((bm, D_MODEL), lambda i, be, nv: (i, 0)),
                  pl.BlockSpec((1, D_MODEL, 2 * D_FF), lambda i, be, nv: (be[i], 0, 0)),
                  pl.BlockSpec((1, 1, 2 * D_FF), lambda i, be, nv: (be[i], 0, 0)),
                  pl.BlockSpec((1, D_FF, D_MODEL), lambda i, be, nv: (be[i], 0, 0)),
                  pl.BlockSpec((1, 1, D_MODEL), lambda i, be, nv: (be[i], 0, 0))],
        out_specs=pl.BlockSpec((bm, D_MODEL), lambda i, be, nv: (i, 0)),
        scratch_shapes=[pltpu.VMEM((D_MODEL, 2 * D_FF), _MXU),
                        pltpu.VMEM((D_FF, D_MODEL), _MXU)],
    )
    return pl.pallas_call(
        _moe_kernel,
        out_shape=jax.ShapeDtypeStruct((nblk * bm, D_MODEL), _F32),
        grid_spec=grid_spec,
        compiler_params=_params("arbitrary"),
        name="moe_experts",
    )(blk_e, nvalid, x_sorted, wgu, bgu, wdn, bdn)


def _alibi_slopes(n):
    def pow2(m):
        start = 2.0 ** (-(2.0 ** -(math.log2(m) - 3)))
        return [start * start ** i for i in range(m)]
    if math.log2(n).is_integer():
        s = pow2(n)
    else:
        c = 2 ** math.floor(math.log2(n))
        s = pow2(c) + pow2(2 * c)[0::2][: n - c]
    return np.array(s, dtype=np.float32)


def _np_split3(x):
    as_mxu = lambda a: np.asarray(a, np.float32).astype(jnp.bfloat16).astype(np.float32)
    p1 = as_mxu(x)
    p2 = as_mxu(x - p1)
    p3 = as_mxu(x - p1 - p2)
    return p1, p2, p3


def _pad_heads(w, n_heads, scale=None):
    k = w.shape[0]
    w = w.reshape(k, n_heads, HEAD_DIM)
    if scale is not None:
        w = w * scale
    return jnp.pad(w, ((0, 0), (0, 0), (0, HEAD_BLOCK - HEAD_DIM))).reshape(k, n_heads * HEAD_BLOCK)


def _pad_heads_alternating(w, n_heads):
    k = w.shape[0]
    w = w.reshape(k, n_heads // 2, 2, HEAD_DIM)
    z = jnp.zeros_like(w[:, :, 0])
    even = jnp.concatenate([w[:, :, 0], z], axis=-1)
    odd = jnp.concatenate([z, w[:, :, 1]], axis=-1)
    return jnp.stack([even, odd], axis=2).reshape(k, n_heads * HEAD_BLOCK)


def _value_ones(n_heads):
    s = np.zeros((LANES, n_heads * HEAD_BLOCK), np.float32)
    for h in range(n_heads):
        s[AUX_ONES, h * HEAD_BLOCK + (HEAD_DIM if h % 2 == 0 else 0)] = 1.0
    return s


def _fox_bias_placement():
    sq = np.zeros((LANES, SELF_AUG), np.float32)
    sk = np.zeros((LANES, SELF_AUG), np.float32)
    for h in range(N_SELF_HEADS):
        base = h * HEAD_BLOCK + HEAD_DIM
        for p in range(3):
            sq[3 * h + p, base + p] = 1.0
            sk[AUX_ONES, base + p] = 1.0
            sq[AUX_ONES, base + 3 + p] = 1.0
            sk[3 * h + p, base + 3 + p] = -1.0
    return sq, sk


def _moba_bias_placement():
    slopes = _np_split3(_alibi_slopes(N_SELF_HEADS))
    sq = np.zeros((LANES, SELF_AUG), np.float32)
    sk = np.zeros((LANES, SELF_AUG), np.float32)
    for h in range(N_SELF_HEADS):
        base = h * HEAD_BLOCK + HEAD_DIM
        for tpart in range(2):
            for p in range(3):
                c = base + 3 * tpart + p
                sq[tpart, c] = 1.0
                sk[AUX_ONES, c] = -slopes[p][h]
                sq[AUX_ONES, c + 6] = slopes[p][h]
                sk[tpart, c + 6] = 1.0
    return sq, sk


def _moba_aux(batch, seq):
    t = np.arange(seq)
    aux = np.zeros((seq, LANES), np.float32)
    aux[:, 0] = (t // 64) * 64
    aux[:, 1] = t % 64
    aux[:, AUX_ONES] = 1.0
    return jnp.asarray(np.tile(aux, (batch, 1)), _MXU)


def _route(top_i, rank, counts, bm, nblk):
    gsz = ((counts + bm - 1) // bm) * bm
    gend = jnp.cumsum(gsz)
    gstart = gend - gsz
    onehot = top_i[:, :, None] == jnp.arange(N_EXPERTS, dtype=jnp.int32)[None, None, :]
    pos = jnp.sum(jnp.where(onehot, gstart[None, None, :], 0), axis=-1) + rank
    blk_start = jnp.arange(nblk, dtype=jnp.int32) * bm
    blk_e = jnp.sum((blk_start[:, None] >= gend[None, :]).astype(jnp.int32), axis=1)
    blk_e = jnp.minimum(blk_e, N_EXPERTS - 1).astype(jnp.int32)
    nvalid = (gend[-1:] // bm).astype(jnp.int32)
    return pos, blk_e, nvalid


def kernel(x, mem, w_in_moba, w_in_fox, b_fgate, w_mem_kv, w_o, ln1_g, ln1_b, router_w, router_b,
           w_gate_up, b_gate_up, w_down, b_down, ln2_g, ln2_b):
    batch, seq, d = x.shape
    depth = w_o.shape[0]
    n = batch * seq
    n_mem = mem.shape[1]
    alpha = (2 * depth) ** 0.25
    bm = 512
    nblk = (n * TOP_K) // bm + N_EXPERTS
    tm_c = 128

    ones_aux = np.zeros((batch * n_mem, LANES), np.float32)
    ones_aux[:, AUX_ONES] = 1.0
    mk, mv = _proj(mem.reshape(batch * n_mem, d), jnp.asarray(ones_aux, _MXU),
                   [_pad_heads(w_mem_kv[:, :MEM_WIDTH], N_MEM_HEADS).astype(_MXU),
                    _pad_heads_alternating(w_mem_kv[:, MEM_WIDTH:], N_MEM_HEADS).astype(_MXU)],
                   [None, jnp.asarray(_value_ones(N_MEM_HEADS), _MXU)])
    mk = mk.reshape(batch, n_mem, MEM_AUG)
    mv = mv.reshape(batch, n_mem, MEM_AUG)

    sv = jnp.asarray(_value_ones(N_SELF_HEADS), _MXU)
    moba_aux = _moba_aux(batch, seq)
    moba_s = [jnp.asarray(s, _MXU) for s in _moba_bias_placement()]
    fox_s = [jnp.asarray(s, _MXU) for s in _fox_bias_placement()]
    pad_lanes = lambda w: jnp.pad(w, ((0, 0), (0, LANES - w.shape[1])))

    h = x.reshape(n, d)
    for i in range(depth):
        j = i // 2
        is_moba = i % 2 == 0
        w_in = w_in_moba[j] if is_moba else w_in_fox[j]
        ws = [_pad_heads(w_in[:, :SELF_WIDTH], N_SELF_HEADS, QK_SCALE).astype(_MXU),
              _pad_heads(w_in[:, SELF_WIDTH:2 * SELF_WIDTH], N_SELF_HEADS).astype(_MXU),
              _pad_heads_alternating(w_in[:, 2 * SELF_WIDTH:3 * SELF_WIDTH], N_SELF_HEADS).astype(_MXU),
              _pad_heads(w_in[:, 3 * SELF_WIDTH:IN_COLS], N_MEM_HEADS, QK_SCALE).astype(_MXU)]
        if is_moba:
            aux = moba_aux
            sq, sk = moba_s
        else:
            aux = _fgate(h, pad_lanes(w_in[:, IN_COLS:]).astype(_MXU), pad_lanes(b_fgate[j][None, :]),
                         batch, seq)
            sq, sk = fox_s
        outs = _proj(h, aux, ws, [sq, sk, sv, None], with_kbar=is_moba)
        qa, ka, va, qm = (o.reshape(batch, seq, -1) for o in outs[:4])
        kbar = outs[4].reshape(batch, seq // MOBA_BLOCK, SELF_AUG) if is_moba else None
        o_self = _attention(qa, ka, va, kbar, causal=True, moba=is_moba, tq=512)
        o_mem = _attention(qm, mk, mv, None, causal=False, moba=False)

        rw = pad_lanes(router_w[i])
        rwh = rw.astype(_MXU)
        rwl = (rw - rwh.astype(_F32)).astype(_MXU)
        rb = jnp.pad(router_b[i][None, :], ((0, 0), (0, LANES - N_EXPERTS)), constant_values=MASK_VALUE)
        h1, ti, tg, rk, cnt = _post(h, o_self.reshape(n, SELF_WIDTH), o_mem.reshape(n, MEM_WIDTH),
                                    w_o[i][:SELF_WIDTH].astype(_MXU), w_o[i][SELF_WIDTH:].astype(_MXU),
                                    ln1_g[i][None, :], ln1_b[i][None, :], rwh, rwl, rb, alpha)

        pos, blk_e, nvalid = _route(ti[:, :TOP_K], rk[:, :TOP_K], cnt[-1, 0, :N_EXPERTS], bm, nblk)
        pos_t = pos.reshape(n // tm_c, tm_c, TOP_K).transpose(0, 2, 1).reshape(n // tm_c, 1, TOP_K * tm_c)
        x_sorted = _dispatch(pos_t, h1, nblk * bm, tm=tm_c)
        y_sorted = _moe_experts(blk_e, nvalid, x_sorted, w_gate_up[i], b_gate_up[i][:, None, :],
                                w_down[i], b_down[i][:, None, :], bm=bm)
        h = _combine(pos_t, h1, tg, y_sorted, ln2_g[i][None, :], ln2_b[i][None, :], alpha, tm=tm_c)
    return h.reshape(batch, seq, d)
```

```python
import functools
import math

import numpy as np
import jax
import jax.numpy as jnp
from jax import lax
from jax.experimental import pallas as pl
from jax.experimental.pallas import tpu as pltpu

D_MODEL = 1024
HEAD_DIM = 64
N_SELF_HEADS = 12
N_MEM_HEADS = 4
SELF_WIDTH = N_SELF_HEADS * HEAD_DIM
MEM_WIDTH = N_MEM_HEADS * HEAD_DIM
IN_COLS = 3 * SELF_WIDTH + MEM_WIDTH
MOBA_BLOCK = 256
MOBA_TOPK = 3
N_EXPERTS = 32
TOP_K = 4
D_FF = D_MODEL
SWIGLU_LIMIT = 7.0
SWIGLU_ALPHA = 1.702
LN_EPS = 1e-5

LANES = 128
HEAD_BLOCK = LANES
SELF_AUG = N_SELF_HEADS * HEAD_BLOCK
MEM_AUG = N_MEM_HEADS * HEAD_BLOCK
AUX_ONES = 36
QK_SCALE = HEAD_DIM ** -0.5
MASK_VALUE = -1e30
VMEM_LIMIT = 56 * 1024 * 1024

_MXU = jnp.bfloat16
_F32 = jnp.float32


def _dot(a, b):
    return jnp.dot(a, b, preferred_element_type=_F32)


def _dot_nt(a, b):
    return lax.dot_general(a, b, (((1,), (1,)), ((), ())), preferred_element_type=_F32)


def _split3(x):
    p1 = x.astype(_MXU)
    r1 = x - p1.astype(_F32)
    p2 = r1.astype(_MXU)
    r2 = r1 - p2.astype(_F32)
    return p1, p2, r2.astype(_MXU)


def _layer_norm(y, g, b):
    mu = jnp.mean(y, axis=-1, keepdims=True)
    yc = y - mu
    var = jnp.mean(yc * yc, axis=-1, keepdims=True)
    return yc * lax.rsqrt(var + LN_EPS) * g + b


def _params(*sem):
    return pltpu.CompilerParams(dimension_semantics=sem, vmem_limit_bytes=VMEM_LIMIT)


def _const_spec(shape):
    nd = len(shape)
    return pl.BlockSpec(shape, lambda *_: (0,) * nd)


def _spread_heads(real, layout):
    lane = lax.broadcasted_iota(jnp.int32, (real.shape[0], LANES), 1)
    low = lane < HEAD_DIM
    blocks = []
    for p in range(real.shape[1] // LANES):
        pair = real[:, p * LANES:(p + 1) * LANES]
        blocks.append(jnp.where(low, pair, 0.0))
        if layout == "low":
            blocks.append(jnp.where(low, pltpu.roll(pair, HEAD_DIM, 1), 0.0))
        else:
            blocks.append(jnp.where(low, 0.0, pair))
    return jnp.concatenate(blocks, axis=1)


def _proj_kernel(*refs, outs, with_kbar):
    n_out = len(outs)
    x_ref, aux_ref, w_ref = refs[0], refs[1], refs[2]
    s_refs = refs[3:3 + n_out]
    o_refs = refs[3 + n_out:3 + 2 * n_out]
    real = _dot(x_ref[...].astype(_MXU), w_ref[...])
    aux = aux_ref[...]
    for i, (start, width, layout, has_aux) in enumerate(outs):
        acc = _spread_heads(real[:, start:start + width], layout)
        if has_aux:
            acc = acc + _dot(aux, s_refs[i][...])
        o_refs[i][...] = acc.astype(o_refs[i].dtype)
        if with_kbar and i == 1:
            kbar_ref = refs[3 + 2 * n_out]
            lane = lax.broadcasted_iota(jnp.int32, (1, acc.shape[1]), 1)
            kbar = jnp.mean(acc, axis=0, keepdims=True)
            kbar_ref[0] = jnp.where(lane % HEAD_BLOCK < HEAD_DIM, kbar, 0.0)


def _proj(x, aux, w, outs, ss, *, with_kbar=False, tm=256):
    m, kdim = x.shape
    ss = [s if s is not None else jnp.zeros((LANES, LANES), _MXU) for s in ss]
    widths = [2 * width for _, width, _, _ in outs]
    out_shape = [jax.ShapeDtypeStruct((m, wd), _MXU) for wd in widths]
    out_specs = [pl.BlockSpec((tm, wd), lambda i: (i, 0)) for wd in widths]
    if with_kbar:
        assert tm == MOBA_BLOCK
        out_shape.append(jax.ShapeDtypeStruct((m // tm, 1, widths[1]), _F32))
        out_specs.append(pl.BlockSpec((1, 1, widths[1]), lambda i: (i, 0, 0)))
    in_specs = [pl.BlockSpec((tm, kdim), lambda i: (i, 0)),
                pl.BlockSpec((tm, LANES), lambda i: (i, 0)),
                _const_spec(w.shape)]
    in_specs += [_const_spec(s.shape) for s in ss]
    return pl.pallas_call(
        functools.partial(_proj_kernel, outs=tuple(outs), with_kbar=with_kbar),
        out_shape=out_shape,
        grid=(m // tm,),
        in_specs=in_specs,
        out_specs=out_specs,
        compiler_params=_params("parallel"),
        name="proj",
    )(x, aux, w, *ss)


def _fgate_kernel(x_ref, w_ref, b_ref, tri_ref, place_ref, ones_ref, aux_ref, carry_ref):
    @pl.when(pl.program_id(1) == 0)
    def _():
        carry_ref[...] = jnp.zeros_like(carry_ref)

    z = _dot(x_ref[...].astype(_MXU), w_ref[...]) + b_ref[...]
    ls = jnp.minimum(z, 0.0) - jnp.log(1.0 + jnp.exp(-jnp.abs(z)))
    tri = tri_ref[...]
    l1, l2, l3 = _split3(ls)
    c = _dot(tri, l1) + _dot(tri, l2) + _dot(tri, l3) + carry_ref[...]
    carry_ref[...] = c[c.shape[0] - 1:c.shape[0], :]
    c1, c2, c3 = _split3(c)
    aux = _dot(c1, place_ref[0]) + _dot(c2, place_ref[1]) + _dot(c3, place_ref[2]) + ones_ref[...]
    aux_ref[...] = aux.astype(aux_ref.dtype)


def _fgate(x, w_fg, b_fg, batch, seq, *, tc=512):
    nt = seq // tc
    tri = jnp.asarray(np.tril(np.ones((tc, tc), np.float32)), _MXU)
    place = np.zeros((3, LANES, LANES), np.float32)
    for h in range(N_SELF_HEADS):
        for p in range(3):
            place[p, h, 3 * h + p] = 1.0
    ones = np.zeros((1, LANES), np.float32)
    ones[0, AUX_ONES] = 1.0
    return pl.pallas_call(
        _fgate_kernel,
        out_shape=jax.ShapeDtypeStruct((batch * seq, LANES), _MXU),
        grid=(batch, nt),
        in_specs=[pl.BlockSpec((tc, D_MODEL), lambda b, i: (b * nt + i, 0)),
                  _const_spec((D_MODEL, LANES)), _const_spec((1, LANES)),
                  _const_spec((tc, tc)), _const_spec((3, LANES, LANES)), _const_spec((1, LANES))],
        out_specs=pl.BlockSpec((tc, LANES), lambda b, i: (b * nt + i, 0)),
        scratch_shapes=[pltpu.VMEM((1, LANES), _F32)],
        compiler_params=_params("parallel", "arbitrary"),
        name="fgate",
    )(x, w_fg, b_fg, tri, jnp.asarray(place, _MXU), jnp.asarray(ones))


def _attn_kernel(*refs, tq, tkc, n_chunks, causal, moba):
    if moba:
        qa_ref, ka_ref, va_ref, kbar_ref, o_ref = refs
    else:
        qa_ref, ka_ref, va_ref, o_ref = refs
    qi = pl.program_id(2)
    lane = lax.broadcasted_iota(jnp.int32, (tq, HEAD_BLOCK), 1)
    col = lax.broadcasted_iota(jnp.int32, (tq, tkc), 1)
    blocks_per_chunk = tkc // MOBA_BLOCK
    own_blk = (qi * tq + lax.broadcasted_iota(jnp.int32, (tq, 1), 0)) // MOBA_BLOCK
    own_blk_t = (qi * tq + lax.broadcasted_iota(jnp.int32, (1, tq), 1)) // MOBA_BLOCK

    heads = []
    for hh in range(2):
        cs = slice(hh * HEAD_BLOCK, (hh + 1) * HEAD_BLOCK)
        q = qa_ref[0, :, cs]
        sel = None
        if moba:
            nb = kbar_ref.shape[1]
            gate = _dot_nt(kbar_ref[0, :, cs], q.astype(_F32))
            blk = lax.broadcasted_iota(jnp.int32, (nb, tq), 0)
            g = jnp.where(blk < own_blk_t, gate, -jnp.inf)
            sel_t = jnp.zeros((nb, tq), _F32)
            for _ in range(MOBA_TOPK):
                mx = jnp.max(g, axis=0, keepdims=True)
                idx = jnp.min(jnp.where(g == mx, blk, nb), axis=0, keepdims=True)
                hit = blk == idx
                sel_t = jnp.where(hit & (mx > -jnp.inf), 1.0, sel_t)
                g = jnp.where(hit, -jnp.inf, g)
            sel = sel_t.T
        heads.append((cs, q, sel))

    def scores(head, c):
        cs, q, _ = head
        start = pl.multiple_of(c * tkc, tkc)
        return _dot_nt(q, ka_ref[0, pl.ds(start, tkc), cs]), va_ref[0, pl.ds(start, tkc), cs]

    def moba_mask(s, sel, c, own_too):
        blk = lax.broadcasted_iota(jnp.int32, sel.shape, 1)
        parts = []
        for u in range(blocks_per_chunk):
            jb = c * blocks_per_chunk + u
            ok = jnp.max(jnp.where(blk == jb, sel, 0.0), axis=1, keepdims=True)
            if own_too:
                ok = jnp.where(jb == own_blk, 1.0, ok)
            parts.append(jnp.where(ok > 0.0, s[:, u * MOBA_BLOCK:(u + 1) * MOBA_BLOCK], MASK_VALUE))
        return parts[0] if len(parts) == 1 else jnp.concatenate(parts, axis=1)

    first = (qi * tq) // tkc if causal else 0
    carry = []
    for head in heads:
        s, v = scores(head, first)
        if causal:
            row = lax.broadcasted_iota(jnp.int32, (tq, tkc), 0)
            s = jnp.where(first * tkc + col <= qi * tq + row, s, MASK_VALUE)
            if moba:
                s = moba_mask(s, head[2], first, True)
        m0 = jnp.max(s, axis=1, keepdims=True)
        carry += [m0, _dot(jnp.exp(s - m0).astype(_MXU), v)]

    def step(c, carry):
        out = []
        for hh, head in enumerate(heads):
            m, acc = carry[2 * hh], carry[2 * hh + 1]
            s, v = scores(head, c)
            if moba:
                s = moba_mask(s, head[2], c, False)
            m_new = jnp.maximum(m, jnp.max(s, axis=1, keepdims=True))
            p = jnp.exp(s - m_new).astype(_MXU)
            out += [m_new, jnp.exp(m - m_new) * acc + _dot(p, v)]
        return tuple(out)

    if causal:
        carry = lax.fori_loop(0, first, step, tuple(carry))
    else:
        carry = lax.fori_loop(1, n_chunks, step, tuple(carry))
    acc_even, acc_odd = carry[1], carry[3]

    l_even = jnp.sum(jnp.where(lane == HEAD_DIM, acc_even, 0.0), axis=1, keepdims=True)
    l_odd = jnp.sum(jnp.where(lane == 0, acc_odd, 0.0), axis=1, keepdims=True)
    out = jnp.where(lane < HEAD_DIM, acc_even / l_even, acc_odd / l_odd)
    o_ref[0] = out.astype(o_ref.dtype)


def _attention(qa, ka, va, kbar, *, causal, moba, tq=256, tkc=1024):
    batch, seq, width = qa.shape
    seq_k = ka.shape[1]
    tkc = min(tkc, seq_k)
    npairs = width // (2 * HEAD_BLOCK)
    assert seq % tq == 0 and seq_k % tkc == 0 and tkc % MOBA_BLOCK == 0
    if causal:
        assert seq == seq_k and tkc % tq == 0
    if moba:
        assert tq % MOBA_BLOCK == 0
    pair = 2 * HEAD_BLOCK
    in_specs = [pl.BlockSpec((1, tq, pair), lambda b, p, i: (b, i, p)),
                pl.BlockSpec((1, seq_k, pair), lambda b, p, i: (b, 0, p)),
                pl.BlockSpec((1, seq_k, pair), lambda b, p, i: (b, 0, p))]
    args = [qa, ka, va]
    if moba:
        in_specs.append(pl.BlockSpec((1, kbar.shape[1], pair), lambda b, p, i: (b, 0, p)))
        args.append(kbar)
    return pl.pallas_call(
        functools.partial(_attn_kernel, tq=tq, tkc=tkc, n_chunks=seq_k // tkc, causal=causal, moba=moba),
        out_shape=jax.ShapeDtypeStruct((batch, seq, npairs * HEAD_BLOCK), _MXU),
        grid=(batch, npairs, seq // tq),
        in_specs=in_specs,
        out_specs=pl.BlockSpec((1, tq, HEAD_BLOCK), lambda b, p, i: (b, i, p)),
        compiler_params=_params("parallel", "parallel", "arbitrary"),
        name="attn_moba" if moba else ("attn_fox" if causal else "attn_mem"),
    )(*args)


def _post_kernel(h_ref, os_ref, om_ref, wo1_ref, wo2_ref, g_ref, b_ref, rwh_ref, rwl_ref, rb_ref, tri_ref,
                 h1_ref, ti_ref, tg_ref, rk_ref, cnt_ref, carry_ref, *, alpha):
    @pl.when(pl.program_id(0) == 0)
    def _():
        carry_ref[...] = jnp.zeros_like(carry_ref)

    y = alpha * h_ref[...] + _dot(os_ref[...], wo1_ref[...]) + _dot(om_ref[...], wo2_ref[...])
    h1 = _layer_norm(y, g_ref[...], b_ref[...])
    h1_ref[...] = h1
    hi = h1.astype(_MXU)
    lo = (h1 - hi.astype(_F32)).astype(_MXU)
    rwh = rwh_ref[...]
    logits = _dot(hi, rwh) + _dot(hi, rwl_ref[...]) + _dot(lo, rwh) + rb_ref[...]
    lane = lax.broadcasted_iota(jnp.int32, logits.shape, 1)
    ti = jnp.zeros(logits.shape, jnp.int32)
    tg = jnp.zeros(logits.shape, _F32)
    chosen = jnp.zeros(logits.shape, _F32)
    top = None
    denom = None
    es, hits = [], []
    for k in range(TOP_K):
        mx = jnp.max(logits, axis=1, keepdims=True)
        idx = jnp.min(jnp.where(logits == mx, lane, LANES), axis=1, keepdims=True)
        hit = lane == idx
        hits.append(hit)
        logits = jnp.where(hit, -jnp.inf, logits)
        chosen = jnp.where(hit, 1.0, chosen)
        ti = jnp.where(lane == k, idx, ti)
        if k == 0:
            top = mx
        e = jnp.exp(mx - top)
        es.append(e)
        denom = e if denom is None else denom + e
    earlier = _dot(tri_ref[...], chosen.astype(_MXU)) + carry_ref[...]
    carry_ref[...] = carry_ref[...] + jnp.sum(chosen, axis=0, keepdims=True)
    rk = jnp.zeros(logits.shape, _F32)
    for k in range(TOP_K):
        tg = jnp.where(lane == k, es[k] / denom, tg)
        rank_k = jnp.sum(jnp.where(hits[k], earlier, 0.0), axis=1, keepdims=True)
        rk = jnp.where(lane == k, rank_k, rk)
    ti_ref[...] = ti
    tg_ref[...] = tg
    rk_ref[...] = rk.astype(jnp.int32)
    cnt_ref[0] = carry_ref[...].astype(jnp.int32)


def _post(h, o_self, o_mem, wo1, wo2, g, b, rwh, rwl, rb, alpha, *, tm=512):
    n = h.shape[0]
    row = lambda w: pl.BlockSpec((tm, w), lambda i: (i, 0))
    tri = jnp.asarray(np.tril(np.ones((tm, tm), np.float32), -1), _MXU)
    return pl.pallas_call(
        functools.partial(_post_kernel, alpha=alpha),
        out_shape=[jax.ShapeDtypeStruct((n, D_MODEL), _F32),
                   jax.ShapeDtypeStruct((n, LANES), jnp.int32),
                   jax.ShapeDtypeStruct((n, LANES), _F32),
                   jax.ShapeDtypeStruct((n, LANES), jnp.int32),
                   jax.ShapeDtypeStruct((n // tm, 1, LANES), jnp.int32)],
        grid=(n // tm,),
        in_specs=[row(D_MODEL), row(SELF_WIDTH), row(MEM_WIDTH),
                  _const_spec(wo1.shape), _const_spec(wo2.shape),
                  _const_spec((1, D_MODEL)), _const_spec((1, D_MODEL)),
                  _const_spec(rwh.shape), _const_spec(rwl.shape), _const_spec((1, LANES)),
                  _const_spec((tm, tm))],
        out_specs=[row(D_MODEL), row(LANES), row(LANES), row(LANES),
                   pl.BlockSpec((1, 1, LANES), lambda i: (i, 0, 0))],
        scratch_shapes=[pltpu.VMEM((1, LANES), _F32)],
        compiler_params=_params("arbitrary"),
        name="post_attn",
    )(h, o_self, o_mem, wo1, wo2, g, b, rwh, rwl, rb, tri)


def _load_indices(idx_vmem_ref, idx_smem, sem):
    to_smem = pltpu.make_async_copy(idx_vmem_ref, idx_smem, sem)
    to_smem.start()
    to_smem.wait()


def _dispatch_kernel(gend_ref, gsz_ref, pos_ref, h1_ref, xs_hbm, zeros, idx_smem, sem, *, tm, bm):
    @pl.when(pl.program_id(0) == 0)
    def _():
        zeros[...] = jnp.zeros_like(zeros)

        def fill(start):
            return pltpu.make_async_copy(zeros, xs_hbm.at[pl.ds(pl.multiple_of(start, bm), bm), :], sem.at[2])

        for e in range(N_EXPERTS):
            pl.when(gsz_ref[e] > 0)(lambda e=e: fill(gend_ref[e] - bm).start())
        for e in range(N_EXPERTS):
            pl.when(gsz_ref[e] > 0)(lambda e=e: fill(gend_ref[e] - bm).wait())

        first_unused = gend_ref[N_EXPERTS - 1] // bm
        n_blocks = xs_hbm.shape[0] // bm

        def start_fill(b, carry):
            fill(b * bm).start()
            return carry

        def wait_fill(b, carry):
            fill(b * bm).wait()
            return carry

        lax.fori_loop(first_unused, n_blocks, start_fill, 0)
        lax.fori_loop(first_unused, n_blocks, wait_fill, 0)

    _load_indices(pos_ref.at[0], idx_smem, sem.at[0])
    for r in range(TOP_K * tm):
        p = idx_smem[0, r]
        row_copy = pltpu.make_async_copy(h1_ref.at[pl.ds(r % tm, 1), :], xs_hbm.at[pl.ds(p, 1), :], sem.at[1])
        row_copy.start(priority=r % 2)
    for k in range(TOP_K):
        pltpu.make_async_copy(h1_ref, xs_hbm.at[pl.ds(0, tm), :], sem.at[1]).wait()


def _dispatch(gend, gsz, pos_t, h1, n_rows, *, tm, bm):
    n = h1.shape[0]
    grid_spec = pltpu.PrefetchScalarGridSpec(
        num_scalar_prefetch=2,
        grid=(n // tm,),
        in_specs=[pl.BlockSpec((1, 1, TOP_K * tm), lambda i, ge, gs: (i, 0, 0)),
                  pl.BlockSpec((tm, D_MODEL), lambda i, ge, gs: (i, 0))],
        out_specs=pl.BlockSpec(memory_space=pl.ANY),
        scratch_shapes=[pltpu.VMEM((bm, D_MODEL), _F32),
                        pltpu.SMEM((1, TOP_K * tm), jnp.int32),
                        pltpu.SemaphoreType.DMA((3,))],
    )
    return pl.pallas_call(
        functools.partial(_dispatch_kernel, tm=tm, bm=bm),
        out_shape=jax.ShapeDtypeStruct((n_rows, D_MODEL), _F32),
        grid_spec=grid_spec,
        compiler_params=_params("arbitrary"),
        name="moe_dispatch",
    )(gend, gsz, pos_t, h1)


def _combine_kernel(pos_ref, h1_ref, tg_ref, y_hbm, g_ref, b_ref, o_ref, ybuf, idx_smem, sem, *, tm, alpha):
    _load_indices(pos_ref.at[0], idx_smem, sem.at[0])

    for r in range(TOP_K * tm):
        p = idx_smem[0, r]
        pltpu.make_async_copy(y_hbm.at[pl.ds(p, 1), :], ybuf.at[pl.ds(r, 1), :], sem.at[1]).start(priority=r % 2)
    pltpu.make_async_copy(y_hbm.at[pl.ds(0, TOP_K * tm), :], ybuf, sem.at[1]).wait()
    tg = tg_ref[...]
    f = alpha * h1_ref[...]
    for k in range(TOP_K):
        f = f + tg[:, k:k + 1] * ybuf[k * tm:(k + 1) * tm, :]
    o_ref[...] = _layer_norm(f, g_ref[...], b_ref[...])


def _combine(pos_t, h1, tg, y_sorted, g, b, alpha, *, tm):
    n = h1.shape[0]
    return pl.pallas_call(
        functools.partial(_combine_kernel, tm=tm, alpha=alpha),
        out_shape=jax.ShapeDtypeStruct((n, D_MODEL), _F32),
        grid=(n // tm,),
        in_specs=[pl.BlockSpec((1, 1, TOP_K * tm), lambda i: (i, 0, 0)),
                  pl.BlockSpec((tm, D_MODEL), lambda i: (i, 0)),
                  pl.BlockSpec((tm, LANES), lambda i: (i, 0)),
                  pl.BlockSpec(memory_space=pl.ANY),
                  _const_spec((1, D_MODEL)), _const_spec((1, D_MODEL))],
        out_specs=pl.BlockSpec((tm, D_MODEL), lambda i: (i, 0)),
        scratch_shapes=[pltpu.VMEM((TOP_K * tm, D_MODEL), _F32),
                        pltpu.SMEM((1, TOP_K * tm), jnp.int32),
                        pltpu.SemaphoreType.DMA((2,))],
        compiler_params=_params("arbitrary"),
        name="moe_combine",
    )(pos_t, h1, tg, y_sorted, g, b)


def _moe_kernel(be_ref, nv_ref, x_ref, wgu_ref, bgu_ref, wdn_ref, bdn_ref, y_ref, wgu_mxu, wdn_mxu):
    i = pl.program_id(0)
    valid = i < nv_ref[0]
    new_expert = jnp.logical_or(i == 0, be_ref[i] != be_ref[jnp.maximum(i - 1, 0)])

    @pl.when(jnp.logical_and(valid, new_expert))
    def _():
        wgu_mxu[...] = wgu_ref[0, 0].astype(_MXU)
        wdn_mxu[...] = wdn_ref[0, 0].astype(_MXU)

    @pl.when(valid)
    def _():
        gu = _dot(x_ref[...].astype(_MXU), wgu_mxu[...]) + bgu_ref[0, 0]
        g = jnp.minimum(gu[:, :D_FF], SWIGLU_LIMIT)
        u = jnp.clip(gu[:, D_FF:], -SWIGLU_LIMIT, SWIGLU_LIMIT)
        a = g * jax.nn.sigmoid(SWIGLU_ALPHA * g) * (u + 1.0)
        y_ref[...] = _dot(a.astype(_MXU), wdn_mxu[...]) + bdn_ref[0, 0]

    @pl.when(jnp.logical_not(valid))
    def _():
        y_ref[...] = jnp.zeros_like(y_ref)


def _moe_experts(blk_e, nvalid, x_sorted, wgu, bgu, wdn, bdn, layer, *, bm):
    nblk = x_sorted.shape[0] // bm
    grid_spec = pltpu.PrefetchScalarGridSpec(
        num_scalar_prefetch=2,
        grid=(nblk,),
        in_specs=[pl.BlockSpec((bm, D_MODEL), lambda i, be, nv: (jnp.minimum(i, nv[0] - 1), 0)),
                  pl.BlockSpec((1, 1, D_MODEL, 2 * D_FF), lambda i, be, nv: (layer, be[i], 0, 0)),
                  pl.BlockSpec((1, 1, 1, 2 * D_FF), lambda i, be, nv: (layer, be[i], 0, 0)),
                  pl.BlockSpec((1, 1, D_FF, D_MODEL), lambda i, be, nv: (layer, be[i], 0, 0)),
                  pl.BlockSpec((1, 1, 1, D_MODEL), lambda i, be, nv: (layer, be[i], 0, 0))],
        out_specs=pl.BlockSpec((bm, D_MODEL), lambda i, be, nv: (i, 0)),
        scratch_shapes=[pltpu.VMEM((D_MODEL, 2 * D_FF), _MXU),
                        pltpu.VMEM((D_FF, D_MODEL), _MXU)],
    )
    return pl.pallas_call(
        _moe_kernel,
        out_shape=jax.ShapeDtypeStruct((nblk * bm, D_MODEL), _F32),
        grid_spec=grid_spec,
        compiler_params=_params("arbitrary"),
        name="moe_experts",
    )(blk_e, nvalid, x_sorted, wgu, bgu, wdn, bdn)


def _alibi_slopes(n):
    def pow2(m):
        start = 2.0 ** (-(2.0 ** -(math.log2(m) - 3)))
        return [start * start ** i for i in range(m)]
    if math.log2(n).is_integer():
        s = pow2(n)
    else:
        c = 2 ** math.floor(math.log2(n))
        s = pow2(c) + pow2(2 * c)[0::2][: n - c]
    return np.array(s, dtype=np.float32)


def _np_split3(x):
    as_mxu = lambda a: np.asarray(a, np.float32).astype(jnp.bfloat16).astype(np.float32)
    p1 = as_mxu(x)
    p2 = as_mxu(x - p1)
    p3 = as_mxu(x - p1 - p2)
    return p1, p2, p3


def _value_ones(n_heads):
    s = np.zeros((LANES, n_heads * HEAD_BLOCK), np.float32)
    for h in range(n_heads):
        s[AUX_ONES, h * HEAD_BLOCK + (HEAD_DIM if h % 2 == 0 else 0)] = 1.0
    return s


def _fox_bias_placement():
    sq = np.zeros((LANES, SELF_AUG), np.float32)
    sk = np.zeros((LANES, SELF_AUG), np.float32)
    for h in range(N_SELF_HEADS):
        base = h * HEAD_BLOCK + HEAD_DIM
        for p in range(3):
            sq[3 * h + p, base + p] = 1.0
            sk[AUX_ONES, base + p] = 1.0
            sq[AUX_ONES, base + 3 + p] = 1.0
            sk[3 * h + p, base + 3 + p] = -1.0
    return sq, sk


def _moba_bias_placement():
    slopes = _np_split3(_alibi_slopes(N_SELF_HEADS))
    sq = np.zeros((LANES, SELF_AUG), np.float32)
    sk = np.zeros((LANES, SELF_AUG), np.float32)
    for h in range(N_SELF_HEADS):
        base = h * HEAD_BLOCK + HEAD_DIM
        for tpart in range(2):
            for p in range(3):
                c = base + 3 * tpart + p
                sq[tpart, c] = 1.0
                sk[AUX_ONES, c] = -slopes[p][h]
                sq[AUX_ONES, c + 6] = slopes[p][h]
                sk[tpart, c + 6] = 1.0
    return sq, sk


def _moba_aux(batch, seq):
    t = np.arange(seq)
    aux = np.zeros((seq, LANES), np.float32)
    aux[:, 0] = (t // 64) * 64
    aux[:, 1] = t % 64
    aux[:, AUX_ONES] = 1.0
    return jnp.asarray(np.tile(aux, (batch, 1)), _MXU)


def _route(top_i, rank, counts, bm, nblk):
    gsz = ((counts + bm - 1) // bm) * bm
    gend = jnp.cumsum(gsz)
    gstart = gend - gsz
    onehot = top_i[:, :, None] == jnp.arange(N_EXPERTS, dtype=jnp.int32)[None, None, :]
    pos = jnp.sum(jnp.where(onehot, gstart[None, None, :], 0), axis=-1) + rank
    blk_start = jnp.arange(nblk, dtype=jnp.int32) * bm
    blk_e = jnp.sum((blk_start[:, None] >= gend[None, :]).astype(jnp.int32), axis=1)
    blk_e = jnp.minimum(blk_e, N_EXPERTS - 1).astype(jnp.int32)
    nvalid = (gend[-1:] // bm).astype(jnp.int32)
    return pos, gend.astype(jnp.int32), gsz.astype(jnp.int32), blk_e, nvalid


def kernel(x, mem, w_in_moba, w_in_fox, b_fgate, w_mem_kv, w_o, ln1_g, ln1_b, router_w, router_b,
           w_gate_up, b_gate_up, w_down, b_down, ln2_g, ln2_b):
    batch, seq, d = x.shape
    depth = w_o.shape[0]
    n = batch * seq
    n_mem = mem.shape[1]
    alpha = (2 * depth) ** 0.25
    bm = 512
    nblk = (n * TOP_K) // bm + N_EXPERTS
    tm_c = 128

    ones_aux = np.zeros((batch * n_mem, LANES), np.float32)
    ones_aux[:, AUX_ONES] = 1.0
    mk, mv = _proj(mem.reshape(batch * n_mem, d), jnp.asarray(ones_aux, _MXU), w_mem_kv.astype(_MXU),
                   [(0, MEM_WIDTH, "low", False), (MEM_WIDTH, MEM_WIDTH, "alternating", True)],
                   [None, jnp.asarray(_value_ones(N_MEM_HEADS), _MXU)])
    mk = mk.reshape(batch, n_mem, MEM_AUG)
    mv = mv.reshape(batch, n_mem, MEM_AUG)

    sv = jnp.asarray(_value_ones(N_SELF_HEADS), _MXU)
    moba_aux = _moba_aux(batch, seq)
    moba_s = [jnp.asarray(s, _MXU) for s in _moba_bias_placement()]
    fox_s = [jnp.asarray(s, _MXU) for s in _fox_bias_placement()]
    pad_lanes = lambda w: jnp.pad(w, ((0, 0), (0, LANES - w.shape[1])))
    col_scale = np.ones((1, IN_COLS), np.float32)
    col_scale[0, :SELF_WIDTH] = QK_SCALE
    col_scale[0, 3 * SELF_WIDTH:] = QK_SCALE
    proj_outs = [(0, SELF_WIDTH, "low", True), (SELF_WIDTH, SELF_WIDTH, "low", True),
                 (2 * SELF_WIDTH, SELF_WIDTH, "alternating", True), (3 * SELF_WIDTH, MEM_WIDTH, "low", False)]
    b_gate_up4 = b_gate_up[:, :, None, :]
    b_down4 = b_down[:, :, None, :]

    h = x.reshape(n, d)
    for i in range(depth):
        j = i // 2
        is_moba = i % 2 == 0
        w_in = w_in_moba[j] if is_moba else w_in_fox[j]
        w_qkv = (w_in[:, :IN_COLS] * col_scale).astype(_MXU)
        if is_moba:
            aux = moba_aux
            sq, sk = moba_s
        else:
            aux = _fgate(h, pad_lanes(w_in[:, IN_COLS:]).astype(_MXU), pad_lanes(b_fgate[j][None, :]),
                         batch, seq)
            sq, sk = fox_s
        outs = _proj(h, aux, w_qkv, proj_outs, [sq, sk, sv, None], with_kbar=is_moba)
        qa, ka, va, qm = (o.reshape(batch, seq, -1) for o in outs[:4])
        kbar = outs[4].reshape(batch, seq // MOBA_BLOCK, SELF_AUG) if is_moba else None
        o_self = _attention(qa, ka, va, kbar, causal=True, moba=is_moba, tq=512)
        o_mem = _attention(qm, mk, mv, None, causal=False, moba=False)

        rw = pad_lanes(router_w[i])
        rwh = rw.astype(_MXU)
        rwl = (rw - rwh.astype(_F32)).astype(_MXU)
        rb = jnp.pad(router_b[i][None, :], ((0, 0), (0, LANES - N_EXPERTS)), constant_values=MASK_VALUE)
        h1, ti, tg, rk, cnt = _post(h, o_self.reshape(n, SELF_WIDTH), o_mem.reshape(n, MEM_WIDTH),
                                    w_o[i][:SELF_WIDTH].astype(_MXU), w_o[i][SELF_WIDTH:].astype(_MXU),
                                    ln1_g[i][None, :], ln1_b[i][None, :], rwh, rwl, rb, alpha)

        pos, gend, gsz, blk_e, nvalid = _route(ti[:, :TOP_K], rk[:, :TOP_K], cnt[-1, 0, :N_EXPERTS], bm, nblk)
        pos_t = pos.reshape(n // tm_c, tm_c, TOP_K).transpose(0, 2, 1).reshape(n // tm_c, 1, TOP_K * tm_c)
        x_sorted = _dispatch(gend, gsz, pos_t, h1, nblk * bm, tm=tm_c, bm=bm)
        y_sorted = _moe_experts(blk_e, nvalid, x_sorted, w_gate_up, b_gate_up4, w_down, b_down4, i, bm=bm)
        h = _combine(pos_t, h1, tg, y_sorted, ln2_g[i][None, :], ln2_b[i][None, :], alpha, tm=tm_c)
    return h.reshape(batch, seq, d)
```

```python
import functools
import math

import numpy as np
import jax
import jax.numpy as jnp
from jax import lax
from jax.experimental import pallas as pl
from jax.experimental.pallas import tpu as pltpu

D_MODEL = 1024
HEAD_DIM = 64
N_SELF_HEADS = 12
N_MEM_HEADS = 4
SELF_WIDTH = N_SELF_HEADS * HEAD_DIM
MEM_WIDTH = N_MEM_HEADS * HEAD_DIM
IN_COLS = 3 * SELF_WIDTH + MEM_WIDTH
MOBA_BLOCK = 256
MOBA_TOPK = 3
N_EXPERTS = 32
TOP_K = 4
D_FF = D_MODEL
SWIGLU_LIMIT = 7.0
SWIGLU_ALPHA = 1.702
LN_EPS = 1e-5

LANES = 128
HEAD_BLOCK = LANES
SELF_AUG = N_SELF_HEADS * HEAD_BLOCK
MEM_AUG = N_MEM_HEADS * HEAD_BLOCK
AUX_ONES = 36
QK_SCALE = HEAD_DIM ** -0.5
LOG2E = math.log2(math.e)
MASK_VALUE = -1e30
VMEM_LIMIT = 56 * 1024 * 1024

_MXU = jnp.bfloat16
_F32 = jnp.float32


def _dot(a, b):
    return jnp.dot(a, b, preferred_element_type=_F32)


def _dot_nt(a, b):
    return lax.dot_general(a, b, (((1,), (1,)), ((), ())), preferred_element_type=_F32)


def _split3(x):
    p1 = x.astype(_MXU)
    r1 = x - p1.astype(_F32)
    p2 = r1.astype(_MXU)
    r2 = r1 - p2.astype(_F32)
    return p1, p2, r2.astype(_MXU)


def _layer_norm(y, g, b):
    mu = jnp.mean(y, axis=-1, keepdims=True)
    yc = y - mu
    var = jnp.mean(yc * yc, axis=-1, keepdims=True)
    return yc * lax.rsqrt(var + LN_EPS) * g + b


def _params(*sem):
    return pltpu.CompilerParams(dimension_semantics=sem, vmem_limit_bytes=VMEM_LIMIT)


def _const_spec(shape):
    nd = len(shape)
    return pl.BlockSpec(shape, lambda *_: (0,) * nd)


def _spread_heads(real, layout):
    lane = lax.broadcasted_iota(jnp.int32, (real.shape[0], LANES), 1)
    low = lane < HEAD_DIM
    blocks = []
    for p in range(real.shape[1] // LANES):
        pair = real[:, p * LANES:(p + 1) * LANES]
        blocks.append(jnp.where(low, pair, 0.0))
        if layout == "low":
            blocks.append(jnp.where(low, pltpu.roll(pair, HEAD_DIM, 1), 0.0))
        else:
            blocks.append(jnp.where(low, 0.0, pair))
    return jnp.concatenate(blocks, axis=1)


def _proj_kernel(*refs, outs, with_kbar):
    n_out = len(outs)
    x_ref, aux_ref, w_ref = refs[0], refs[1], refs[2]
    s_refs = refs[3:3 + n_out]
    o_refs = refs[3 + n_out:3 + 2 * n_out]
    real = _dot(x_ref[...].astype(_MXU), w_ref[...])
    aux = aux_ref[...]
    for i, (start, width, layout, has_aux) in enumerate(outs):
        acc = _spread_heads(real[:, start:start + width], layout)
        if has_aux:
            acc = acc + _dot(aux, s_refs[i][...])
        o_refs[i][...] = acc.astype(o_refs[i].dtype)
        if with_kbar and i == 1:
            kbar_ref = refs[3 + 2 * n_out]
            lane = lax.broadcasted_iota(jnp.int32, (1, acc.shape[1]), 1)
            kbar = jnp.mean(acc, axis=0, keepdims=True)
            kbar_ref[0] = jnp.where(lane % HEAD_BLOCK < HEAD_DIM, kbar, 0.0)


def _proj(x, aux, w, outs, ss, *, with_kbar=False, tm=256):
    m, kdim = x.shape
    ss = [s if s is not None else jnp.zeros((LANES, LANES), _MXU) for s in ss]
    widths = [2 * width for _, width, _, _ in outs]
    out_shape = [jax.ShapeDtypeStruct((m, wd), _MXU) for wd in widths]
    out_specs = [pl.BlockSpec((tm, wd), lambda i: (i, 0)) for wd in widths]
    if with_kbar:
        assert tm == MOBA_BLOCK
        out_shape.append(jax.ShapeDtypeStruct((m // tm, 1, widths[1]), _F32))
        out_specs.append(pl.BlockSpec((1, 1, widths[1]), lambda i: (i, 0, 0)))
    in_specs = [pl.BlockSpec((tm, kdim), lambda i: (i, 0)),
                pl.BlockSpec((tm, LANES), lambda i: (i, 0)),
                _const_spec(w.shape)]
    in_specs += [_const_spec(s.shape) for s in ss]
    return pl.pallas_call(
        functools.partial(_proj_kernel, outs=tuple(outs), with_kbar=with_kbar),
        out_shape=out_shape,
        grid=(m // tm,),
        in_specs=in_specs,
        out_specs=out_specs,
        compiler_params=_params("parallel"),
        name="proj",
    )(x, aux, w, *ss)


def _fgate_kernel(x_ref, w_ref, b_ref, tri_ref, place_ref, ones_ref, aux_ref, carry_ref):
    @pl.when(pl.program_id(1) == 0)
    def _():
        carry_ref[...] = jnp.zeros_like(carry_ref)

    z = _dot(x_ref[...].astype(_MXU), w_ref[...]) + b_ref[...]
    ls = jnp.minimum(z, 0.0) - jnp.log(1.0 + jnp.exp(-jnp.abs(z)))
    tri = tri_ref[...]
    l1, l2, l3 = _split3(ls)
    c = _dot(tri, l1) + _dot(tri, l2) + _dot(tri, l3) + carry_ref[...]
    carry_ref[...] = c[c.shape[0] - 1:c.shape[0], :]
    c1, c2, c3 = _split3(c * LOG2E)
    aux = _dot(c1, place_ref[0]) + _dot(c2, place_ref[1]) + _dot(c3, place_ref[2]) + ones_ref[...]
    aux_ref[...] = aux.astype(aux_ref.dtype)


def _fgate(x, w_fg, b_fg, batch, seq, *, tc=512):
    nt = seq // tc
    tri = jnp.asarray(np.tril(np.ones((tc, tc), np.float32)), _MXU)
    place = np.zeros((3, LANES, LANES), np.float32)
    for h in range(N_SELF_HEADS):
        for p in range(3):
            place[p, h, 3 * h + p] = 1.0
    ones = np.zeros((1, LANES), np.float32)
    ones[0, AUX_ONES] = 1.0
    return pl.pallas_call(
        _fgate_kernel,
        out_shape=jax.ShapeDtypeStruct((batch * seq, LANES), _MXU),
        grid=(batch, nt),
        in_specs=[pl.BlockSpec((tc, D_MODEL), lambda b, i: (b * nt + i, 0)),
                  _const_spec((D_MODEL, LANES)), _const_spec((1, LANES)),
                  _const_spec((tc, tc)), _const_spec((3, LANES, LANES)), _const_spec((1, LANES))],
        out_specs=pl.BlockSpec((tc, LANES), lambda b, i: (b * nt + i, 0)),
        scratch_shapes=[pltpu.VMEM((1, LANES), _F32)],
        compiler_params=_params("parallel", "arbitrary"),
        name="fgate",
    )(x, w_fg, b_fg, tri, jnp.asarray(place, _MXU), jnp.asarray(ones))


def _attn_kernel(*refs, tq, tkc, n_chunks, causal, moba):
    if moba:
        qa_ref, ka_ref, va_ref, kbar_ref, o_ref = refs
    else:
        qa_ref, ka_ref, va_ref, o_ref = refs
    qi = pl.program_id(2)
    lane = lax.broadcasted_iota(jnp.int32, (tq, HEAD_BLOCK), 1)
    col = lax.broadcasted_iota(jnp.int32, (tq, tkc), 1)
    blocks_per_chunk = tkc // MOBA_BLOCK
    own_blk = (qi * tq + lax.broadcasted_iota(jnp.int32, (tq, 1), 0)) // MOBA_BLOCK
    own_blk_t = (qi * tq + lax.broadcasted_iota(jnp.int32, (1, tq), 1)) // MOBA_BLOCK

    heads = []
    for hh in range(2):
        cs = slice(hh * HEAD_BLOCK, (hh + 1) * HEAD_BLOCK)
        q = qa_ref[0, :, cs]
        sel = None
        if moba:
            nb = kbar_ref.shape[1]
            gate = _dot_nt(kbar_ref[0, :, cs], q.astype(_F32))
            blk = lax.broadcasted_iota(jnp.int32, (nb, tq), 0)
            g = jnp.where(blk < own_blk_t, gate, -jnp.inf)
            sel_t = jnp.zeros((nb, tq), _F32)
            for _ in range(MOBA_TOPK):
                mx = jnp.max(g, axis=0, keepdims=True)
                idx = jnp.min(jnp.where(g == mx, blk, nb), axis=0, keepdims=True)
                hit = blk == idx
                sel_t = jnp.where(hit & (mx > -jnp.inf), 1.0, sel_t)
                g = jnp.where(hit, -jnp.inf, g)
            sel = sel_t.T
        heads.append((cs, q, sel))

    def scores(head, c):
        cs, q, _ = head
        return _dot_nt(q, ka_ref[0, pl.ds(pl.multiple_of(c * tkc, tkc), tkc), cs])

    def values(head, c):
        return va_ref[0, pl.ds(pl.multiple_of(c * tkc, tkc), tkc), head[0]]

    def online(m, acc, s, v):
        m_new = jnp.maximum(m, jnp.max(s, axis=1, keepdims=True))
        p = jnp.exp2(s - m_new).astype(_MXU)
        return m_new, jnp.exp2(m - m_new) * acc + _dot(p, v)

    def moba_mask(s, sel, c, own_too):
        blk = lax.broadcasted_iota(jnp.int32, sel.shape, 1)
        parts = []
        for u in range(blocks_per_chunk):
            jb = c * blocks_per_chunk + u
            ok = jnp.max(jnp.where(blk == jb, sel, 0.0), axis=1, keepdims=True)
            if own_too:
                ok = jnp.where(jb == own_blk, 1.0, ok)
            parts.append(jnp.where(ok > 0.0, s[:, u * MOBA_BLOCK:(u + 1) * MOBA_BLOCK], MASK_VALUE))
        return parts[0] if len(parts) == 1 else jnp.concatenate(parts, axis=1)

    first = (qi * tq) // tkc if causal else 0
    carry = []
    for head in heads:
        s = scores(head, first)
        if causal:
            row = lax.broadcasted_iota(jnp.int32, (tq, tkc), 0)
            s = jnp.where(first * tkc + col <= qi * tq + row, s, MASK_VALUE)
            if moba:
                s = moba_mask(s, head[2], first, True)
        m0 = jnp.max(s, axis=1, keepdims=True)
        carry += [m0, _dot(jnp.exp2(s - m0).astype(_MXU), values(head, first))]

    def step(c, carry):
        out = []
        for hh, head in enumerate(heads):
            s = scores(head, c)
            if moba:
                s = moba_mask(s, head[2], c, False)
            out += list(online(carry[2 * hh], carry[2 * hh + 1], s, values(head, c)))
        return tuple(out)

    if causal:
        pairs = first // 2
        carry = lax.fori_loop(0, pairs, lambda j, c: step(2 * j + 1, step(2 * j, c)), tuple(carry))
        carry = lax.fori_loop(2 * pairs, first, step, carry)
    else:
        carry = lax.fori_loop(1, n_chunks, step, tuple(carry))
    acc_even, acc_odd = carry[1], carry[3]

    l_even = jnp.sum(jnp.where(lane == HEAD_DIM, acc_even, 0.0), axis=1, keepdims=True)
    l_odd = jnp.sum(jnp.where(lane == 0, acc_odd, 0.0), axis=1, keepdims=True)
    out = jnp.where(lane < HEAD_DIM, acc_even / l_even, acc_odd / l_odd)
    o_ref[0] = out.astype(o_ref.dtype)


def _attention(qa, ka, va, kbar, *, causal, moba, tq=256, tkc=1024):
    batch, seq, width = qa.shape
    seq_k = ka.shape[1]
    tkc = min(tkc, seq_k)
    npairs = width // (2 * HEAD_BLOCK)
    assert seq % tq == 0 and seq_k % tkc == 0 and tkc % MOBA_BLOCK == 0
    if causal:
        assert seq == seq_k and tkc % tq == 0
    if moba:
        assert tq % MOBA_BLOCK == 0
    pair = 2 * HEAD_BLOCK
    in_specs = [pl.BlockSpec((1, tq, pair), lambda b, p, i: (b, i, p)),
                pl.BlockSpec((1, seq_k, pair), lambda b, p, i: (b, 0, p)),
                pl.BlockSpec((1, seq_k, pair), lambda b, p, i: (b, 0, p))]
    args = [qa, ka, va]
    if moba:
        in_specs.append(pl.BlockSpec((1, kbar.shape[1], pair), lambda b, p, i: (b, 0, p)))
        args.append(kbar)
    return pl.pallas_call(
        functools.partial(_attn_kernel, tq=tq, tkc=tkc, n_chunks=seq_k // tkc, causal=causal, moba=moba),
        out_shape=jax.ShapeDtypeStruct((batch, seq, npairs * HEAD_BLOCK), _MXU),
        grid=(batch, npairs, seq // tq),
        in_specs=in_specs,
        out_specs=pl.BlockSpec((1, tq, HEAD_BLOCK), lambda b, p, i: (b, i, p)),
        compiler_params=_params("parallel", "parallel", "arbitrary"),
        name="attn_moba" if moba else ("attn_fox" if causal else "attn_mem"),
    )(*args)


def _post_kernel(h_ref, os_ref, om_ref, wo1_ref, wo2_ref, g_ref, b_ref, rwh_ref, rwl_ref, rb_ref, tri_ref,
                 h1_ref, ti_ref, tg_ref, rk_ref, cnt_ref, carry_ref, *, alpha):
    @pl.when(pl.program_id(0) == 0)
    def _():
        carry_ref[...] = jnp.zeros_like(carry_ref)

    y = alpha * h_ref[...] + _dot(os_ref[...], wo1_ref[...]) + _dot(om_ref[...], wo2_ref[...])
    h1 = _layer_norm(y, g_ref[...], b_ref[...])
    h1_ref[...] = h1
    hi = h1.astype(_MXU)
    lo = (h1 - hi.astype(_F32)).astype(_MXU)
    rwh = rwh_ref[...]
    logits = _dot(hi, rwh) + _dot(hi, rwl_ref[...]) + _dot(lo, rwh) + rb_ref[...]
    lane = lax.broadcasted_iota(jnp.int32, logits.shape, 1)
    ti = jnp.zeros(logits.shape, jnp.int32)
    tg = jnp.zeros(logits.shape, _F32)
    chosen = jnp.zeros(logits.shape, _F32)
    top = None
    denom = None
    es, hits = [], []
    for k in range(TOP_K):
        mx = jnp.max(logits, axis=1, keepdims=True)
        idx = jnp.min(jnp.where(logits == mx, lane, LANES), axis=1, keepdims=True)
        hit = lane == idx
        hits.append(hit)
        logits = jnp.where(hit, -jnp.inf, logits)
        chosen = jnp.where(hit, 1.0, chosen)
        ti = jnp.where(lane == k, idx, ti)
        if k == 0:
            top = mx
        e = jnp.exp(mx - top)
        es.append(e)
        denom = e if denom is None else denom + e
    earlier = _dot(tri_ref[...], chosen.astype(_MXU)) + carry_ref[...]
    carry_ref[...] = carry_ref[...] + jnp.sum(chosen, axis=0, keepdims=True)
    rk = jnp.zeros(logits.shape, _F32)
    for k in range(TOP_K):
        tg = jnp.where(lane == k, es[k] / denom, tg)
        rank_k = jnp.sum(jnp.where(hits[k], earlier, 0.0), axis=1, keepdims=True)
        rk = jnp.where(lane == k, rank_k, rk)
    ti_ref[...] = ti
    tg_ref[...] = tg
    rk_ref[...] = rk.astype(jnp.int32)
    cnt_ref[0] = carry_ref[...].astype(jnp.int32)


def _post(h, o_self, o_mem, wo1, wo2, g, b, rwh, rwl, rb, alpha, *, tm=512):
    n = h.shape[0]
    row = lambda w: pl.BlockSpec((tm, w), lambda i: (i, 0))
    tri = jnp.asarray(np.tril(np.ones((tm, tm), np.float32), -1), _MXU)
    return pl.pallas_call(
        functools.partial(_post_kernel, alpha=alpha),
        out_shape=[jax.ShapeDtypeStruct((n, D_MODEL), _F32),
                   jax.ShapeDtypeStruct((n, LANES), jnp.int32),
                   jax.ShapeDtypeStruct((n, LANES), _F32),
                   jax.ShapeDtypeStruct((n, LANES), jnp.int32),
                   jax.ShapeDtypeStruct((n // tm, 1, LANES), jnp.int32)],
        grid=(n // tm,),
        in_specs=[row(D_MODEL), row(SELF_WIDTH), row(MEM_WIDTH),
                  _const_spec(wo1.shape), _const_spec(wo2.shape),
                  _const_spec((1, D_MODEL)), _const_spec((1, D_MODEL)),
                  _const_spec(rwh.shape), _const_spec(rwl.shape), _const_spec((1, LANES)),
                  _const_spec((tm, tm))],
        out_specs=[row(D_MODEL), row(LANES), row(LANES), row(LANES),
                   pl.BlockSpec((1, 1, LANES), lambda i: (i, 0, 0))],
        scratch_shapes=[pltpu.VMEM((1, LANES), _F32)],
        compiler_params=_params("arbitrary"),
        name="post_attn",
    )(h, o_self, o_mem, wo1, wo2, g, b, rwh, rwl, rb, tri)


def _load_indices(idx_vmem_ref, idx_smem, sem):
    to_smem = pltpu.make_async_copy(idx_vmem_ref, idx_smem, sem)
    to_smem.start()
    to_smem.wait()


def _dispatch_kernel(gend_ref, gsz_ref, pos_ref, h1_ref, xs_hbm, zeros, idx_smem, sem, *, tm, bm):
    @pl.when(pl.program_id(0) == 0)
    def _():
        zeros[...] = jnp.zeros_like(zeros)

        def fill(start):
            return pltpu.make_async_copy(zeros, xs_hbm.at[pl.ds(pl.multiple_of(start, bm), bm), :], sem.at[2])

        for e in range(N_EXPERTS):
            pl.when(gsz_ref[e] > 0)(lambda e=e: fill(gend_ref[e] - bm).start())
        for e in range(N_EXPERTS):
            pl.when(gsz_ref[e] > 0)(lambda e=e: fill(gend_ref[e] - bm).wait())

        first_unused = gend_ref[N_EXPERTS - 1] // bm
        n_blocks = xs_hbm.shape[0] // bm

        def start_fill(b, carry):
            fill(b * bm).start()
            return carry

        def wait_fill(b, carry):
            fill(b * bm).wait()
            return carry

        lax.fori_loop(first_unused, n_blocks, start_fill, 0)
        lax.fori_loop(first_unused, n_blocks, wait_fill, 0)

    _load_indices(pos_ref.at[0], idx_smem, sem.at[0])
    for r in range(TOP_K * tm):
        p = idx_smem[0, r]
        row_copy = pltpu.make_async_copy(h1_ref.at[pl.ds(r % tm, 1), :], xs_hbm.at[pl.ds(p, 1), :], sem.at[1])
        row_copy.start(priority=r % 2)
    for k in range(TOP_K):
        pltpu.make_async_copy(h1_ref, xs_hbm.at[pl.ds(0, tm), :], sem.at[1]).wait()


def _dispatch(gend, gsz, pos_t, h1, n_rows, *, tm, bm):
    n = h1.shape[0]
    grid_spec = pltpu.PrefetchScalarGridSpec(
        num_scalar_prefetch=2,
        grid=(n // tm,),
        in_specs=[pl.BlockSpec((1, 1, TOP_K * tm), lambda i, ge, gs: (i, 0, 0)),
                  pl.BlockSpec((tm, D_MODEL), lambda i, ge, gs: (i, 0))],
        out_specs=pl.BlockSpec(memory_space=pl.ANY),
        scratch_shapes=[pltpu.VMEM((bm, D_MODEL), _F32),
                        pltpu.SMEM((1, TOP_K * tm), jnp.int32),
                        pltpu.SemaphoreType.DMA((3,))],
    )
    return pl.pallas_call(
        functools.partial(_dispatch_kernel, tm=tm, bm=bm),
        out_shape=jax.ShapeDtypeStruct((n_rows, D_MODEL), _F32),
        grid_spec=grid_spec,
        compiler_params=_params("arbitrary"),
        name="moe_dispatch",
    )(gend, gsz, pos_t, h1)


def _combine_kernel(pos_ref, h1_ref, tg_ref, y_hbm, g_ref, b_ref, o_ref, ybuf, idx_smem, sem, *, tm, alpha):
    _load_indices(pos_ref.at[0], idx_smem, sem.at[0])

    for r in range(TOP_K * tm):
        p = idx_smem[0, r]
        pltpu.make_async_copy(y_hbm.at[pl.ds(p, 1), :], ybuf.at[pl.ds(r, 1), :], sem.at[1]).start(priority=r % 2)
    pltpu.make_async_copy(y_hbm.at[pl.ds(0, TOP_K * tm), :], ybuf, sem.at[1]).wait()
    tg = tg_ref[...]
    f = alpha * h1_ref[...]
    for k in range(TOP_K):
        f = f + tg[:, k:k + 1] * ybuf[k * tm:(k + 1) * tm, :]
    o_ref[...] = _layer_norm(f, g_ref[...], b_ref[...])


def _combine(pos_t, h1, tg, y_sorted, g, b, alpha, *, tm):
    n = h1.shape[0]
    return pl.pallas_call(
        functools.partial(_combine_kernel, tm=tm, alpha=alpha),
        out_shape=jax.ShapeDtypeStruct((n, D_MODEL), _F32),
        grid=(n // tm,),
        in_specs=[pl.BlockSpec((1, 1, TOP_K * tm), lambda i: (i, 0, 0)),
                  pl.BlockSpec((tm, D_MODEL), lambda i: (i, 0)),
                  pl.BlockSpec((tm, LANES), lambda i: (i, 0)),
                  pl.BlockSpec(memory_space=pl.ANY),
                  _const_spec((1, D_MODEL)), _const_spec((1, D_MODEL))],
        out_specs=pl.BlockSpec((tm, D_MODEL), lambda i: (i, 0)),
        scratch_shapes=[pltpu.VMEM((TOP_K * tm, D_MODEL), _F32),
                        pltpu.SMEM((1, TOP_K * tm), jnp.int32),
                        pltpu.SemaphoreType.DMA((2,))],
        compiler_params=_params("arbitrary"),
        name="moe_combine",
    )(pos_t, h1, tg, y_sorted, g, b)


def _moe_kernel(be_ref, nv_ref, x_ref, wgu_ref, bgu_ref, wdn_ref, bdn_ref, y_ref, wgu_mxu, wdn_mxu):
    i = pl.program_id(0)
    valid = i < nv_ref[0]
    new_expert = jnp.logical_or(i == 0, be_ref[i] != be_ref[jnp.maximum(i - 1, 0)])

    @pl.when(jnp.logical_and(valid, new_expert))
    def _():
        wgu_mxu[...] = wgu_ref[0, 0].astype(_MXU)
        wdn_mxu[...] = wdn_ref[0, 0].astype(_MXU)

    @pl.when(valid)
    def _():
        gu = _dot(x_ref[...].astype(_MXU), wgu_mxu[...]) + bgu_ref[0, 0]
        g = jnp.minimum(gu[:, :D_FF], SWIGLU_LIMIT)
        u = jnp.clip(gu[:, D_FF:], -SWIGLU_LIMIT, SWIGLU_LIMIT)
        a = g * jax.nn.sigmoid(SWIGLU_ALPHA * g) * (u + 1.0)
        y_ref[...] = _dot(a.astype(_MXU), wdn_mxu[...]) + bdn_ref[0, 0]

    @pl.when(jnp.logical_not(valid))
    def _():
        y_ref[...] = jnp.zeros_like(y_ref)


def _moe_experts(blk_e, nvalid, x_sorted, wgu, bgu, wdn, bdn, layer, *, bm):
    nblk = x_sorted.shape[0] // bm
    grid_spec = pltpu.PrefetchScalarGridSpec(
        num_scalar_prefetch=2,
        grid=(nblk,),
        in_specs=[pl.BlockSpec((bm, D_MODEL), lambda i, be, nv: (jnp.minimum(i, nv[0] - 1), 0)),
                  pl.BlockSpec((1, 1, D_MODEL, 2 * D_FF), lambda i, be, nv: (layer, be[i], 0, 0)),
                  pl.BlockSpec((1, 1, 1, 2 * D_FF), lambda i, be, nv: (layer, be[i], 0, 0)),
                  pl.BlockSpec((1, 1, D_FF, D_MODEL), lambda i, be, nv: (layer, be[i], 0, 0)),
                  pl.BlockSpec((1, 1, 1, D_MODEL), lambda i, be, nv: (layer, be[i], 0, 0))],
        out_specs=pl.BlockSpec((bm, D_MODEL), lambda i, be, nv: (i, 0)),
        scratch_shapes=[pltpu.VMEM((D_MODEL, 2 * D_FF), _MXU),
                        pltpu.VMEM((D_FF, D_MODEL), _MXU)],
    )
    return pl.pallas_call(
        _moe_kernel,
        out_shape=jax.ShapeDtypeStruct((nblk * bm, D_MODEL), _F32),
        grid_spec=grid_spec,
        compiler_params=_params("arbitrary"),
        name="moe_experts",
    )(blk_e, nvalid, x_sorted, wgu, bgu, wdn, bdn)


def _alibi_slopes(n):
    def pow2(m):
        start = 2.0 ** (-(2.0 ** -(math.log2(m) - 3)))
        return [start * start ** i for i in range(m)]
    if math.log2(n).is_integer():
        s = pow2(n)
    else:
        c = 2 ** math.floor(math.log2(n))
        s = pow2(c) + pow2(2 * c)[0::2][: n - c]
    return np.array(s, dtype=np.float32)


def _np_split3(x):
    as_mxu = lambda a: np.asarray(a, np.float32).astype(jnp.bfloat16).astype(np.float32)
    p1 = as_mxu(x)
    p2 = as_mxu(x - p1)
    p3 = as_mxu(x - p1 - p2)
    return p1, p2, p3


def _value_ones(n_heads):
    s = np.zeros((LANES, n_heads * HEAD_BLOCK), np.float32)
    for h in range(n_heads):
        s[AUX_ONES, h * HEAD_BLOCK + (HEAD_DIM if h % 2 == 0 else 0)] = 1.0
    return s


def _fox_bias_placement():
    sq = np.zeros((LANES, SELF_AUG), np.float32)
    sk = np.zeros((LANES, SELF_AUG), np.float32)
    for h in range(N_SELF_HEADS):
        base = h * HEAD_BLOCK + HEAD_DIM
        for p in range(3):
            sq[3 * h + p, base + p] = 1.0
            sk[AUX_ONES, base + p] = 1.0
            sq[AUX_ONES, base + 3 + p] = 1.0
            sk[3 * h + p, base + 3 + p] = -1.0
    return sq, sk


def _moba_bias_placement():
    slopes = _np_split3(_alibi_slopes(N_SELF_HEADS) * np.float32(LOG2E))
    sq = np.zeros((LANES, SELF_AUG), np.float32)
    sk = np.zeros((LANES, SELF_AUG), np.float32)
    for h in range(N_SELF_HEADS):
        base = h * HEAD_BLOCK + HEAD_DIM
        for tpart in range(2):
            for p in range(3):
                c = base + 3 * tpart + p
                sq[tpart, c] = 1.0
                sk[AUX_ONES, c] = -slopes[p][h]
                sq[AUX_ONES, c + 6] = slopes[p][h]
                sk[tpart, c + 6] = 1.0
    return sq, sk


def _moba_aux(batch, seq):
    t = np.arange(seq)
    aux = np.zeros((seq, LANES), np.float32)
    aux[:, 0] = (t // 64) * 64
    aux[:, 1] = t % 64
    aux[:, AUX_ONES] = 1.0
    return jnp.asarray(np.tile(aux, (batch, 1)), _MXU)


def _route(top_i, rank, counts, bm, nblk):
    gsz = ((counts + bm - 1) // bm) * bm
    gend = jnp.cumsum(gsz)
    gstart = gend - gsz
    onehot = top_i[:, :, None] == jnp.arange(N_EXPERTS, dtype=jnp.int32)[None, None, :]
    pos = jnp.sum(jnp.where(onehot, gstart[None, None, :], 0), axis=-1) + rank
    blk_start = jnp.arange(nblk, dtype=jnp.int32) * bm
    blk_e = jnp.sum((blk_start[:, None] >= gend[None, :]).astype(jnp.int32), axis=1)
    blk_e = jnp.minimum(blk_e, N_EXPERTS - 1).astype(jnp.int32)
    nvalid = (gend[-1:] // bm).astype(jnp.int32)
    return pos, gend.astype(jnp.int32), gsz.astype(jnp.int32), blk_e, nvalid


def kernel(x, mem, w_in_moba, w_in_fox, b_fgate, w_mem_kv, w_o, ln1_g, ln1_b, router_w, router_b,
           w_gate_up, b_gate_up, w_down, b_down, ln2_g, ln2_b):
    batch, seq, d = x.shape
    depth = w_o.shape[0]
    n = batch * seq
    n_mem = mem.shape[1]
    alpha = (2 * depth) ** 0.25
    bm = 512
    nblk = (n * TOP_K) // bm + N_EXPERTS
    tm_c = 256

    ones_aux = np.zeros((batch * n_mem, LANES), np.float32)
    ones_aux[:, AUX_ONES] = 1.0
    mk, mv = _proj(mem.reshape(batch * n_mem, d), jnp.asarray(ones_aux, _MXU), w_mem_kv.astype(_MXU),
                   [(0, MEM_WIDTH, "low", False), (MEM_WIDTH, MEM_WIDTH, "alternating", True)],
                   [None, jnp.asarray(_value_ones(N_MEM_HEADS), _MXU)])
    mk = mk.reshape(batch, n_mem, MEM_AUG)
    mv = mv.reshape(batch, n_mem, MEM_AUG)

    sv = jnp.asarray(_value_ones(N_SELF_HEADS), _MXU)
    moba_aux = _moba_aux(batch, seq)
    moba_s = [jnp.asarray(s, _MXU) for s in _moba_bias_placement()]
    fox_s = [jnp.asarray(s, _MXU) for s in _fox_bias_placement()]
    pad_lanes = lambda w: jnp.pad(w, ((0, 0), (0, LANES - w.shape[1])))
    col_scale = np.ones((1, IN_COLS), np.float32)
    col_scale[0, :SELF_WIDTH] = QK_SCALE * LOG2E
    col_scale[0, 3 * SELF_WIDTH:] = QK_SCALE * LOG2E
    proj_outs = [(0, SELF_WIDTH, "low", True), (SELF_WIDTH, SELF_WIDTH, "low", True),
                 (2 * SELF_WIDTH, SELF_WIDTH, "alternating", True), (3 * SELF_WIDTH, MEM_WIDTH, "low", False)]
    b_gate_up4 = b_gate_up[:, :, None, :]
    b_down4 = b_down[:, :, None, :]

    h = x.reshape(n, d)
    for i in range(depth):
        j = i // 2
        is_moba = i % 2 == 0
        w_in = w_in_moba[j] if is_moba else w_in_fox[j]
        w_qkv = (w_in[:, :IN_COLS] * col_scale).astype(_MXU)
        if is_moba:
            aux = moba_aux
            sq, sk = moba_s
        else:
            aux = _fgate(h, pad_lanes(w_in[:, IN_COLS:]).astype(_MXU), pad_lanes(b_fgate[j][None, :]),
                         batch, seq)
            sq, sk = fox_s
        outs = _proj(h, aux, w_qkv, proj_outs, [sq, sk, sv, None], with_kbar=is_moba)
        qa, ka, va, qm = (o.reshape(batch, seq, -1) for o in outs[:4])
        kbar = outs[4].reshape(batch, seq // MOBA_BLOCK, SELF_AUG) if is_moba else None
        o_self = _attention(qa, ka, va, kbar, causal=True, moba=is_moba, tq=512)
        o_mem = _attention(qm, mk, mv, None, causal=False, moba=False, tq=512)

        rw = pad_lanes(router_w[i])
        rwh = rw.astype(_MXU)
        rwl = (rw - rwh.astype(_F32)).astype(_MXU)
        rb = jnp.pad(router_b[i][None, :], ((0, 0), (0, LANES - N_EXPERTS)), constant_values=MASK_VALUE)
        h1, ti, tg, rk, cnt = _post(h, o_self.reshape(n, SELF_WIDTH), o_mem.reshape(n, MEM_WIDTH),
                                    w_o[i][:SELF_WIDTH].astype(_MXU), w_o[i][SELF_WIDTH:].astype(_MXU),
                                    ln1_g[i][None, :], ln1_b[i][None, :], rwh, rwl, rb, alpha)

        pos, gend, gsz, blk_e, nvalid = _route(ti[:, :TOP_K], rk[:, :TOP_K], cnt[-1, 0, :N_EXPERTS], bm, nblk)
        pos_t = pos.reshape(n // tm_c, tm_c, TOP_K).transpose(0, 2, 1).reshape(n // tm_c, 1, TOP_K * tm_c)
        x_sorted = _dispatch(gend, gsz, pos_t, h1, nblk * bm, tm=tm_c, bm=bm)
        y_sorted = _moe_experts(blk_e, nvalid, x_sorted, w_gate_up, b_gate_up4, w_down, b_down4, i, bm=bm)
        h = _combine(pos_t, h1, tg, y_sorted, ln2_g[i][None, :], ln2_b[i][None, :], alpha, tm=tm_c)
    return h.reshape(batch, seq, d)
```

```python
import functools
import math

import numpy as np
import jax
import jax.numpy as jnp
from jax import lax
from jax.experimental import pallas as pl
from jax.experimental.pallas import tpu as pltpu

D_MODEL = 1024
HEAD_DIM = 64
N_SELF_HEADS = 12
N_MEM_HEADS = 4
SELF_WIDTH = N_SELF_HEADS * HEAD_DIM
MEM_WIDTH = N_MEM_HEADS * HEAD_DIM
IN_COLS = 3 * SELF_WIDTH + MEM_WIDTH
MOBA_BLOCK = 256
MOBA_TOPK = 3
N_EXPERTS = 32
TOP_K = 4
D_FF = D_MODEL
SWIGLU_LIMIT = 7.0
SWIGLU_ALPHA = 1.702
LN_EPS = 1e-5

LANES = 128
HEAD_BLOCK = LANES
SELF_AUG = N_SELF_HEADS * HEAD_BLOCK
MEM_AUG = N_MEM_HEADS * HEAD_BLOCK
AUX_ONES = 36
QK_SCALE = HEAD_DIM ** -0.5
LOG2E = math.log2(math.e)
MASK_VALUE = -1e30
VMEM_LIMIT = 56 * 1024 * 1024

_MXU = jnp.bfloat16
_F32 = jnp.float32


def _dot(a, b):
    return jnp.dot(a, b, preferred_element_type=_F32)


def _dot_nt(a, b):
    return lax.dot_general(a, b, (((1,), (1,)), ((), ())), preferred_element_type=_F32)


def _split3(x):
    p1 = x.astype(_MXU)
    r1 = x - p1.astype(_F32)
    p2 = r1.astype(_MXU)
    r2 = r1 - p2.astype(_F32)
    return p1, p2, r2.astype(_MXU)


def _layer_norm(y, g, b):
    mu = jnp.mean(y, axis=-1, keepdims=True)
    yc = y - mu
    var = jnp.mean(yc * yc, axis=-1, keepdims=True)
    return yc * lax.rsqrt(var + LN_EPS) * g + b


def _params(*sem):
    return pltpu.CompilerParams(dimension_semantics=sem, vmem_limit_bytes=VMEM_LIMIT)


def _const_spec(shape):
    nd = len(shape)
    return pl.BlockSpec(shape, lambda *_: (0,) * nd)


def _spread_heads(real, layout):
    lane = lax.broadcasted_iota(jnp.int32, (real.shape[0], LANES), 1)
    low = lane < HEAD_DIM
    blocks = []
    for p in range(real.shape[1] // LANES):
        pair = real[:, p * LANES:(p + 1) * LANES]
        blocks.append(jnp.where(low, pair, 0.0))
        if layout == "low":
            blocks.append(jnp.where(low, pltpu.roll(pair, HEAD_DIM, 1), 0.0))
        else:
            blocks.append(jnp.where(low, 0.0, pair))
    return jnp.concatenate(blocks, axis=1)


def _proj_kernel(*refs, outs, with_kbar):
    n_out = len(outs)
    x_ref, aux_ref, w_ref = refs[0], refs[1], refs[2]
    s_refs = refs[3:3 + n_out]
    o_refs = refs[3 + n_out:3 + 2 * n_out]
    real = _dot(x_ref[...].astype(_MXU), w_ref[...])
    aux = aux_ref[...]
    for i, (start, width, layout, has_aux) in enumerate(outs):
        acc = _spread_heads(real[:, start:start + width], layout)
        if has_aux:
            acc = acc + _dot(aux, s_refs[i][...])
        o_refs[i][...] = acc.astype(o_refs[i].dtype)
        if with_kbar and i == 1:
            kbar_ref = refs[3 + 2 * n_out]
            lane = lax.broadcasted_iota(jnp.int32, (1, acc.shape[1]), 1)
            kbar = jnp.mean(acc, axis=0, keepdims=True)
            kbar_ref[0] = jnp.where(lane % HEAD_BLOCK < HEAD_DIM, kbar, 0.0)


def _proj(x, aux, w, outs, ss, *, with_kbar=False, tm=256):
    m, kdim = x.shape
    ss = [s if s is not None else jnp.zeros((LANES, LANES), _MXU) for s in ss]
    widths = [2 * width for _, width, _, _ in outs]
    out_shape = [jax.ShapeDtypeStruct((m, wd), _MXU) for wd in widths]
    out_specs = [pl.BlockSpec((tm, wd), lambda i: (i, 0)) for wd in widths]
    if with_kbar:
        assert tm == MOBA_BLOCK
        out_shape.append(jax.ShapeDtypeStruct((m // tm, 1, widths[1]), _F32))
        out_specs.append(pl.BlockSpec((1, 1, widths[1]), lambda i: (i, 0, 0)))
    in_specs = [pl.BlockSpec((tm, kdim), lambda i: (i, 0)),
                pl.BlockSpec((tm, LANES), lambda i: (i, 0)),
                _const_spec(w.shape)]
    in_specs += [_const_spec(s.shape) for s in ss]
    return pl.pallas_call(
        functools.partial(_proj_kernel, outs=tuple(outs), with_kbar=with_kbar),
        out_shape=out_shape,
        grid=(m // tm,),
        in_specs=in_specs,
        out_specs=out_specs,
        compiler_params=_params("parallel"),
        name="proj",
    )(x, aux, w, *ss)


def _fgate_kernel(x_ref, w_ref, b_ref, tri_ref, place_ref, ones_ref, aux_ref, carry_ref):
    @pl.when(pl.program_id(1) == 0)
    def _():
        carry_ref[...] = jnp.zeros_like(carry_ref)

    z = _dot(x_ref[...].astype(_MXU), w_ref[...]) + b_ref[...]
    ls = jnp.minimum(z, 0.0) - jnp.log(1.0 + jnp.exp(-jnp.abs(z)))
    tri = tri_ref[...]
    l1, l2, l3 = _split3(ls)
    c = _dot(tri, l1) + _dot(tri, l2) + _dot(tri, l3) + carry_ref[...]
    carry_ref[...] = c[c.shape[0] - 1:c.shape[0], :]
    c1, c2, c3 = _split3(c * LOG2E)
    aux = _dot(c1, place_ref[0]) + _dot(c2, place_ref[1]) + _dot(c3, place_ref[2]) + ones_ref[...]
    aux_ref[...] = aux.astype(aux_ref.dtype)


def _fgate(x, w_fg, b_fg, batch, seq, *, tc=512):
    nt = seq // tc
    tri = jnp.asarray(np.tril(np.ones((tc, tc), np.float32)), _MXU)
    place = np.zeros((3, LANES, LANES), np.float32)
    for h in range(N_SELF_HEADS):
        for p in range(3):
            place[p, h, 3 * h + p] = 1.0
    ones = np.zeros((1, LANES), np.float32)
    ones[0, AUX_ONES] = 1.0
    return pl.pallas_call(
        _fgate_kernel,
        out_shape=jax.ShapeDtypeStruct((batch * seq, LANES), _MXU),
        grid=(batch, nt),
        in_specs=[pl.BlockSpec((tc, D_MODEL), lambda b, i: (b * nt + i, 0)),
                  _const_spec((D_MODEL, LANES)), _const_spec((1, LANES)),
                  _const_spec((tc, tc)), _const_spec((3, LANES, LANES)), _const_spec((1, LANES))],
        out_specs=pl.BlockSpec((tc, LANES), lambda b, i: (b * nt + i, 0)),
        scratch_shapes=[pltpu.VMEM((1, LANES), _F32)],
        compiler_params=_params("parallel", "arbitrary"),
        name="fgate",
    )(x, w_fg, b_fg, tri, jnp.asarray(place, _MXU), jnp.asarray(ones))


def _attn_kernel(*refs, tq, tkc, n_chunks, causal, moba):
    if moba:
        qa_ref, ka_ref, va_ref, kbar_ref, o_ref = refs
    else:
        qa_ref, ka_ref, va_ref, o_ref = refs
    qi = pl.program_id(2)
    lane = lax.broadcasted_iota(jnp.int32, (tq, HEAD_BLOCK), 1)
    col = lax.broadcasted_iota(jnp.int32, (tq, tkc), 1)
    blocks_per_chunk = tkc // MOBA_BLOCK
    own_blk = (qi * tq + lax.broadcasted_iota(jnp.int32, (tq, 1), 0)) // MOBA_BLOCK
    own_blk_t = (qi * tq + lax.broadcasted_iota(jnp.int32, (1, tq), 1)) // MOBA_BLOCK

    heads = []
    for hh in range(2):
        cs = slice(hh * HEAD_BLOCK, (hh + 1) * HEAD_BLOCK)
        q = qa_ref[0, :, cs]
        sel = None
        if moba:
            nb = kbar_ref.shape[1]
            gate = _dot_nt(kbar_ref[0, :, cs], q.astype(_F32))
            blk = lax.broadcasted_iota(jnp.int32, (nb, tq), 0)
            g = jnp.where(blk < own_blk_t, gate, -jnp.inf)
            sel_t = jnp.zeros((nb, tq), _F32)
            for _ in range(MOBA_TOPK):
                mx = jnp.max(g, axis=0, keepdims=True)
                idx = jnp.min(jnp.where(g == mx, blk, nb), axis=0, keepdims=True)
                hit = blk == idx
                sel_t = jnp.where(hit & (mx > -jnp.inf), 1.0, sel_t)
                g = jnp.where(hit, -jnp.inf, g)
            sel = sel_t.T
        heads.append((cs, q, sel))

    def scores(head, c):
        cs, q, _ = head
        return _dot_nt(q, ka_ref[0, pl.ds(pl.multiple_of(c * tkc, tkc), tkc), cs])

    def values(head, c):
        return va_ref[0, pl.ds(pl.multiple_of(c * tkc, tkc), tkc), head[0]]

    def online(m, acc, s, v):
        m_new = jnp.maximum(m, jnp.max(s, axis=1, keepdims=True))
        p = jnp.exp2(s - m_new).astype(_MXU)
        return m_new, jnp.exp2(m - m_new) * acc + _dot(p, v)

    def moba_mask(s, sel, c, own_too):
        blk = lax.broadcasted_iota(jnp.int32, sel.shape, 1)
        parts = []
        for u in range(blocks_per_chunk):
            jb = c * blocks_per_chunk + u
            ok = jnp.max(jnp.where(blk == jb, sel, 0.0), axis=1, keepdims=True)
            if own_too:
                ok = jnp.where(jb == own_blk, 1.0, ok)
            parts.append(jnp.where(ok > 0.0, s[:, u * MOBA_BLOCK:(u + 1) * MOBA_BLOCK], MASK_VALUE))
        return parts[0] if len(parts) == 1 else jnp.concatenate(parts, axis=1)

    first = (qi * tq) // tkc if causal else 0
    carry = []
    for head in heads:
        s = scores(head, first)
        if causal:
            row = lax.broadcasted_iota(jnp.int32, (tq, tkc), 0)
            s = jnp.where(first * tkc + col <= qi * tq + row, s, MASK_VALUE)
            if moba:
                s = moba_mask(s, head[2], first, True)
        m0 = jnp.max(s, axis=1, keepdims=True)
        carry += [m0, _dot(jnp.exp2(s - m0).astype(_MXU), values(head, first))]

    def step(c, carry):
        out = []
        for hh, head in enumerate(heads):
            s = scores(head, c)
            if moba:
                s = moba_mask(s, head[2], c, False)
            out += list(online(carry[2 * hh], carry[2 * hh + 1], s, values(head, c)))
        return tuple(out)

    if causal:
        pairs = first // 2
        carry = lax.fori_loop(0, pairs, lambda j, c: step(2 * j + 1, step(2 * j, c)), tuple(carry))
        carry = lax.fori_loop(2 * pairs, first, step, carry)
    else:
        carry = lax.fori_loop(1, n_chunks, step, tuple(carry))
    acc_even, acc_odd = carry[1], carry[3]

    l_even = jnp.sum(jnp.where(lane == HEAD_DIM, acc_even, 0.0), axis=1, keepdims=True)
    l_odd = jnp.sum(jnp.where(lane == 0, acc_odd, 0.0), axis=1, keepdims=True)
    out = jnp.where(lane < HEAD_DIM, acc_even / l_even, acc_odd / l_odd)
    o_ref[0] = out.astype(o_ref.dtype)


def _attention(qa, ka, va, kbar, *, causal, moba, tq=256, tkc=1024):
    batch, seq, width = qa.shape
    seq_k = ka.shape[1]
    tkc = min(tkc, seq_k)
    npairs = width // (2 * HEAD_BLOCK)
    assert seq % tq == 0 and seq_k % tkc == 0 and tkc % MOBA_BLOCK == 0
    if causal:
        assert seq == seq_k and tkc % tq == 0
    if moba:
        assert tq % MOBA_BLOCK == 0
    pair = 2 * HEAD_BLOCK
    in_specs = [pl.BlockSpec((1, tq, pair), lambda b, p, i: (b, i, p)),
                pl.BlockSpec((1, seq_k, pair), lambda b, p, i: (b, 0, p)),
                pl.BlockSpec((1, seq_k, pair), lambda b, p, i: (b, 0, p))]
    args = [qa, ka, va]
    if moba:
        in_specs.append(pl.BlockSpec((1, kbar.shape[1], pair), lambda b, p, i: (b, 0, p)))
        args.append(kbar)
    return pl.pallas_call(
        functools.partial(_attn_kernel, tq=tq, tkc=tkc, n_chunks=seq_k // tkc, causal=causal, moba=moba),
        out_shape=jax.ShapeDtypeStruct((batch, seq, npairs * HEAD_BLOCK), _MXU),
        grid=(batch, npairs, seq // tq),
        in_specs=in_specs,
        out_specs=pl.BlockSpec((1, tq, HEAD_BLOCK), lambda b, p, i: (b, i, p)),
        compiler_params=_params("parallel", "parallel", "arbitrary"),
        name="attn_moba" if moba else ("attn_fox" if causal else "attn_mem"),
    )(*args)


def _post_kernel(h_ref, os_ref, om_ref, wo1_ref, wo2_ref, g_ref, b_ref, rwh_ref, rwl_ref, rb_ref, tri_ref,
                 h1_ref, ti_ref, tg_ref, rk_ref, cnt_ref, carry_ref, *, alpha):
    @pl.when(pl.program_id(0) == 0)
    def _():
        carry_ref[...] = jnp.zeros_like(carry_ref)

    y = alpha * h_ref[...] + _dot(os_ref[...], wo1_ref[...]) + _dot(om_ref[...], wo2_ref[...])
    h1 = _layer_norm(y, g_ref[...], b_ref[...])
    h1_ref[...] = h1
    hi = h1.astype(_MXU)
    lo = (h1 - hi.astype(_F32)).astype(_MXU)
    rwh = rwh_ref[...]
    logits = _dot(hi, rwh) + _dot(hi, rwl_ref[...]) + _dot(lo, rwh) + rb_ref[...]
    lane = lax.broadcasted_iota(jnp.int32, logits.shape, 1)
    ti = jnp.zeros(logits.shape, jnp.int32)
    tg = jnp.zeros(logits.shape, _F32)
    chosen = jnp.zeros(logits.shape, _F32)
    top = None
    denom = None
    es, hits = [], []
    for k in range(TOP_K):
        mx = jnp.max(logits, axis=1, keepdims=True)
        idx = jnp.min(jnp.where(logits == mx, lane, LANES), axis=1, keepdims=True)
        hit = lane == idx
        hits.append(hit)
        logits = jnp.where(hit, -jnp.inf, logits)
        chosen = jnp.where(hit, 1.0, chosen)
        ti = jnp.where(lane == k, idx, ti)
        if k == 0:
            top = mx
        e = jnp.exp(mx - top)
        es.append(e)
        denom = e if denom is None else denom + e
    earlier = _dot(tri_ref[...], chosen.astype(_MXU)) + carry_ref[...]
    carry_ref[...] = carry_ref[...] + jnp.sum(chosen, axis=0, keepdims=True)
    rk = jnp.zeros(logits.shape, _F32)
    for k in range(TOP_K):
        tg = jnp.where(lane == k, es[k] / denom, tg)
        rank_k = jnp.sum(jnp.where(hits[k], earlier, 0.0), axis=1, keepdims=True)
        rk = jnp.where(lane == k, rank_k, rk)
    ti_ref[...] = ti
    tg_ref[...] = tg
    rk_ref[...] = rk.astype(jnp.int32)
    cnt_ref[0] = carry_ref[...].astype(jnp.int32)


def _post(h, o_self, o_mem, wo1, wo2, g, b, rwh, rwl, rb, alpha, *, tm=512):
    n = h.shape[0]
    row = lambda w: pl.BlockSpec((tm, w), lambda i: (i, 0))
    tri = jnp.asarray(np.tril(np.ones((tm, tm), np.float32), -1), _MXU)
    return pl.pallas_call(
        functools.partial(_post_kernel, alpha=alpha),
        out_shape=[jax.ShapeDtypeStruct((n, D_MODEL), _F32),
                   jax.ShapeDtypeStruct((n, LANES), jnp.int32),
                   jax.ShapeDtypeStruct((n, LANES), _F32),
                   jax.ShapeDtypeStruct((n, LANES), jnp.int32),
                   jax.ShapeDtypeStruct((n // tm, 1, LANES), jnp.int32)],
        grid=(n // tm,),
        in_specs=[row(D_MODEL), row(SELF_WIDTH), row(MEM_WIDTH),
                  _const_spec(wo1.shape), _const_spec(wo2.shape),
                  _const_spec((1, D_MODEL)), _const_spec((1, D_MODEL)),
                  _const_spec(rwh.shape), _const_spec(rwl.shape), _const_spec((1, LANES)),
                  _const_spec((tm, tm))],
        out_specs=[row(D_MODEL), row(LANES), row(LANES), row(LANES),
                   pl.BlockSpec((1, 1, LANES), lambda i: (i, 0, 0))],
        scratch_shapes=[pltpu.VMEM((1, LANES), _F32)],
        compiler_params=_params("arbitrary"),
        name="post_attn",
    )(h, o_self, o_mem, wo1, wo2, g, b, rwh, rwl, rb, tri)


def _load_indices(idx_vmem_ref, idx_smem, sem):
    to_smem = pltpu.make_async_copy(idx_vmem_ref, idx_smem, sem)
    to_smem.start()
    to_smem.wait()


def _dispatch_kernel(gend_ref, gsz_ref, pos_ref, h1_ref, xs_hbm, zeros, idx_smem, sem, *, tm, bm):
    @pl.when(pl.program_id(0) == 0)
    def _():
        zeros[...] = jnp.zeros_like(zeros)

        def fill(start):
            return pltpu.make_async_copy(zeros, xs_hbm.at[pl.ds(pl.multiple_of(start, bm), bm), :], sem.at[2])

        for e in range(N_EXPERTS):
            pl.when(gsz_ref[e] > 0)(lambda e=e: fill(gend_ref[e] - bm).start())
        for e in range(N_EXPERTS):
            pl.when(gsz_ref[e] > 0)(lambda e=e: fill(gend_ref[e] - bm).wait())

        first_unused = gend_ref[N_EXPERTS - 1] // bm
        n_blocks = xs_hbm.shape[0] // bm

        def start_fill(b, carry):
            fill(b * bm).start()
            return carry

        def wait_fill(b, carry):
            fill(b * bm).wait()
            return carry

        lax.fori_loop(first_unused, n_blocks, start_fill, 0)
        lax.fori_loop(first_unused, n_blocks, wait_fill, 0)

    _load_indices(pos_ref.at[0], idx_smem, sem.at[0])
    for r in range(TOP_K * tm):
        p = idx_smem[0, r]
        row_copy = pltpu.make_async_copy(h1_ref.at[pl.ds(r % tm, 1), :], xs_hbm.at[pl.ds(p, 1), :], sem.at[1])
        row_copy.start(priority=r % 2)
    for k in range(TOP_K):
        pltpu.make_async_copy(h1_ref, xs_hbm.at[pl.ds(0, tm), :], sem.at[1]).wait()


def _dispatch(gend, gsz, pos_t, h1, n_rows, *, tm, bm):
    n = h1.shape[0]
    grid_spec = pltpu.PrefetchScalarGridSpec(
        num_scalar_prefetch=2,
        grid=(n // tm,),
        in_specs=[pl.BlockSpec((1, 1, TOP_K * tm), lambda i, ge, gs: (i, 0, 0)),
                  pl.BlockSpec((tm, D_MODEL), lambda i, ge, gs: (i, 0))],
        out_specs=pl.BlockSpec(memory_space=pl.ANY),
        scratch_shapes=[pltpu.VMEM((bm, D_MODEL), _F32),
                        pltpu.SMEM((1, TOP_K * tm), jnp.int32),
                        pltpu.SemaphoreType.DMA((3,))],
    )
    return pl.pallas_call(
        functools.partial(_dispatch_kernel, tm=tm, bm=bm),
        out_shape=jax.ShapeDtypeStruct((n_rows, D_MODEL), _F32),
        grid_spec=grid_spec,
        compiler_params=_params("arbitrary"),
        name="moe_dispatch",
    )(gend, gsz, pos_t, h1)


def _combine_kernel(pos_ref, h1_ref, tg_ref, y_hbm, g_ref, b_ref, o_ref, ybuf, idx_smem, sem, *, tm, alpha):
    _load_indices(pos_ref.at[0], idx_smem, sem.at[0])

    for r in range(TOP_K * tm):
        p = idx_smem[0, r]
        pltpu.make_async_copy(y_hbm.at[pl.ds(p, 1), :], ybuf.at[pl.ds(r, 1), :], sem.at[1]).start(priority=r % 2)
    pltpu.make_async_copy(y_hbm.at[pl.ds(0, TOP_K * tm), :], ybuf, sem.at[1]).wait()
    tg = tg_ref[...]
    f = alpha * h1_ref[...]
    for k in range(TOP_K):
        f = f + tg[:, k:k + 1] * ybuf[k * tm:(k + 1) * tm, :]
    o_ref[...] = _layer_norm(f, g_ref[...], b_ref[...])


def _combine(pos_t, h1, tg, y_sorted, g, b, alpha, *, tm):
    n = h1.shape[0]
    return pl.pallas_call(
        functools.partial(_combine_kernel, tm=tm, alpha=alpha),
        out_shape=jax.ShapeDtypeStruct((n, D_MODEL), _F32),
        grid=(n // tm,),
        in_specs=[pl.BlockSpec((1, 1, TOP_K * tm), lambda i: (i, 0, 0)),
                  pl.BlockSpec((tm, D_MODEL), lambda i: (i, 0)),
                  pl.BlockSpec((tm, LANES), lambda i: (i, 0)),
                  pl.BlockSpec(memory_space=pl.ANY),
                  _const_spec((1, D_MODEL)), _const_spec((1, D_MODEL))],
        out_specs=pl.BlockSpec((tm, D_MODEL), lambda i: (i, 0)),
        scratch_shapes=[pltpu.VMEM((TOP_K * tm, D_MODEL), _F32),
                        pltpu.SMEM((1, TOP_K * tm), jnp.int32),
                        pltpu.SemaphoreType.DMA((2,))],
        compiler_params=_params("arbitrary"),
        name="moe_combine",
    )(pos_t, h1, tg, y_sorted, g, b)


def _moe_kernel(be_ref, nv_ref, x_ref, wgu_ref, bgu_ref, wdn_ref, bdn_ref, y_ref, wgu_mxu, wdn_mxu):
    i = pl.program_id(0)
    valid = i < nv_ref[0]
    new_expert = jnp.logical_or(i == 0, be_ref[i] != be_ref[jnp.maximum(i - 1, 0)])

    @pl.when(jnp.logical_and(valid, new_expert))
    def _():
        wgu_mxu[...] = wgu_ref[0, 0].astype(_MXU)
        wdn_mxu[...] = wdn_ref[0, 0].astype(_MXU)

    @pl.when(valid)
    def _():
        gu = _dot(x_ref[...].astype(_MXU), wgu_mxu[...]) + bgu_ref[0, 0]
        g = jnp.minimum(gu[:, :D_FF], SWIGLU_LIMIT)
        u = jnp.clip(gu[:, D_FF:], -SWIGLU_LIMIT, SWIGLU_LIMIT)
        a = g * jax.nn.sigmoid(SWIGLU_ALPHA * g) * (u + 1.0)
        y_ref[...] = _dot(a.astype(_MXU), wdn_mxu[...]) + bdn_ref[0, 0]

    @pl.when(jnp.logical_not(valid))
    def _():
        y_ref[...] = jnp.zeros_like(y_ref)


def _moe_experts(blk_e, nvalid, x_sorted, wgu, bgu, wdn, bdn, layer, *, bm):
    nblk = x_sorted.shape[0] // bm
    grid_spec = pltpu.PrefetchScalarGridSpec(
        num_scalar_prefetch=2,
        grid=(nblk,),
        in_specs=[pl.BlockSpec((bm, D_MODEL), lambda i, be, nv: (jnp.minimum(i, nv[0] - 1), 0)),
                  pl.BlockSpec((1, 1, D_MODEL, 2 * D_FF), lambda i, be, nv: (layer, be[i], 0, 0)),
                  pl.BlockSpec((1, 1, 1, 2 * D_FF), lambda i, be, nv: (layer, be[i], 0, 0)),
                  pl.BlockSpec((1, 1, D_FF, D_MODEL), lambda i, be, nv: (layer, be[i], 0, 0)),
                  pl.BlockSpec((1, 1, 1, D_MODEL), lambda i, be, nv: (layer, be[i], 0, 0))],
        out_specs=pl.BlockSpec((bm, D_MODEL), lambda i, be, nv: (i, 0)),
        scratch_shapes=[pltpu.VMEM((D_MODEL, 2 * D_FF), _MXU),
                        pltpu.VMEM((D_FF, D_MODEL), _MXU)],
    )
    return pl.pallas_call(
        _moe_kernel,
        out_shape=jax.ShapeDtypeStruct((nblk * bm, D_MODEL), _F32),
        grid_spec=grid_spec,
        compiler_params=_params("arbitrary"),
        name="moe_experts",
    )(blk_e, nvalid, x_sorted, wgu, bgu, wdn, bdn)


def _alibi_slopes(n):
    def pow2(m):
        start = 2.0 ** (-(2.0 ** -(math.log2(m) - 3)))
        return [start * start ** i for i in range(m)]
    if math.log2(n).is_integer():
        s = pow2(n)
    else:
        c = 2 ** math.floor(math.log2(n))
        s = pow2(c) + pow2(2 * c)[0::2][: n - c]
    return np.array(s, dtype=np.float32)


def _np_split3(x):
    as_mxu = lambda a: np.asarray(a, np.float32).astype(jnp.bfloat16).astype(np.float32)
    p1 = as_mxu(x)
    p2 = as_mxu(x - p1)
    p3 = as_mxu(x - p1 - p2)
    return p1, p2, p3


def _value_ones(n_heads):
    s = np.zeros((LANES, n_heads * HEAD_BLOCK), np.float32)
    for h in range(n_heads):
        s[AUX_ONES, h * HEAD_BLOCK + (HEAD_DIM if h % 2 == 0 else 0)] = 1.0
    return s


def _fox_bias_placement():
    sq = np.zeros((LANES, SELF_AUG), np.float32)
    sk = np.zeros((LANES, SELF_AUG), np.float32)
    for h in range(N_SELF_HEADS):
        base = h * HEAD_BLOCK + HEAD_DIM
        for p in range(3):
            sq[3 * h + p, base + p] = 1.0
            sk[AUX_ONES, base + p] = 1.0
            sq[AUX_ONES, base + 3 + p] = 1.0
            sk[3 * h + p, base + 3 + p] = -1.0
    return sq, sk


def _moba_bias_placement():
    slopes = _np_split3(_alibi_slopes(N_SELF_HEADS) * np.float32(LOG2E))
    sq = np.zeros((LANES, SELF_AUG), np.float32)
    sk = np.zeros((LANES, SELF_AUG), np.float32)
    for h in range(N_SELF_HEADS):
        base = h * HEAD_BLOCK + HEAD_DIM
        for tpart in range(2):
            for p in range(3):
                c = base + 3 * tpart + p
                sq[tpart, c] = 1.0
                sk[AUX_ONES, c] = -slopes[p][h]
                sq[AUX_ONES, c + 6] = slopes[p][h]
                sk[tpart, c + 6] = 1.0
    return sq, sk


def _moba_aux(batch, seq):
    t = np.arange(seq)
    aux = np.zeros((seq, LANES), np.float32)
    aux[:, 0] = (t // 64) * 64
    aux[:, 1] = t % 64
    aux[:, AUX_ONES] = 1.0
    return jnp.asarray(np.tile(aux, (batch, 1)), _MXU)


def _route(top_i, rank, counts, bm, nblk):
    gsz = ((counts + bm - 1) // bm) * bm
    gend = jnp.cumsum(gsz)
    gstart = gend - gsz
    onehot = top_i[:, :, None] == jnp.arange(N_EXPERTS, dtype=jnp.int32)[None, None, :]
    pos = jnp.sum(jnp.where(onehot, gstart[None, None, :], 0), axis=-1) + rank
    blk_start = jnp.arange(nblk, dtype=jnp.int32) * bm
    blk_e = jnp.sum((blk_start[:, None] >= gend[None, :]).astype(jnp.int32), axis=1)
    blk_e = jnp.minimum(blk_e, N_EXPERTS - 1).astype(jnp.int32)
    nvalid = (gend[-1:] // bm).astype(jnp.int32)
    return pos, gend.astype(jnp.int32), gsz.astype(jnp.int32), blk_e, nvalid


def kernel(x, mem, w_in_moba, w_in_fox, b_fgate, w_mem_kv, w_o, ln1_g, ln1_b, router_w, router_b,
           w_gate_up, b_gate_up, w_down, b_down, ln2_g, ln2_b):
    batch, seq, d = x.shape
    depth = w_o.shape[0]
    n = batch * seq
    n_mem = mem.shape[1]
    alpha = (2 * depth) ** 0.25
    bm = 512
    nblk = (n * TOP_K) // bm + N_EXPERTS
    tm_c = 512

    ones_aux = np.zeros((batch * n_mem, LANES), np.float32)
    ones_aux[:, AUX_ONES] = 1.0
    mk, mv = _proj(mem.reshape(batch * n_mem, d), jnp.asarray(ones_aux, _MXU), w_mem_kv.astype(_MXU),
                   [(0, MEM_WIDTH, "low", False), (MEM_WIDTH, MEM_WIDTH, "alternating", True)],
                   [None, jnp.asarray(_value_ones(N_MEM_HEADS), _MXU)])
    mk = mk.reshape(batch, n_mem, MEM_AUG)
    mv = mv.reshape(batch, n_mem, MEM_AUG)

    sv = jnp.asarray(_value_ones(N_SELF_HEADS), _MXU)
    moba_aux = _moba_aux(batch, seq)
    moba_s = [jnp.asarray(s, _MXU) for s in _moba_bias_placement()]
    fox_s = [jnp.asarray(s, _MXU) for s in _fox_bias_placement()]
    pad_lanes = lambda w: jnp.pad(w, ((0, 0), (0, LANES - w.shape[1])))
    col_scale = np.ones((1, IN_COLS), np.float32)
    col_scale[0, :SELF_WIDTH] = QK_SCALE * LOG2E
    col_scale[0, 3 * SELF_WIDTH:] = QK_SCALE * LOG2E
    proj_outs = [(0, SELF_WIDTH, "low", True), (SELF_WIDTH, SELF_WIDTH, "low", True),
                 (2 * SELF_WIDTH, SELF_WIDTH, "alternating", True), (3 * SELF_WIDTH, MEM_WIDTH, "low", False)]
    b_gate_up4 = b_gate_up[:, :, None, :]
    b_down4 = b_down[:, :, None, :]

    h = x.reshape(n, d)
    for i in range(depth):
        j = i // 2
        is_moba = i % 2 == 0
        w_in = w_in_moba[j] if is_moba else w_in_fox[j]
        w_qkv = (w_in[:, :IN_COLS] * col_scale).astype(_MXU)
        if is_moba:
            aux = moba_aux
            sq, sk = moba_s
        else:
            aux = _fgate(h, pad_lanes(w_in[:, IN_COLS:]).astype(_MXU), pad_lanes(b_fgate[j][None, :]),
                         batch, seq)
            sq, sk = fox_s
        outs = _proj(h, aux, w_qkv, proj_outs, [sq, sk, sv, None], with_kbar=is_moba)
        qa, ka, va, qm = (o.reshape(batch, seq, -1) for o in outs[:4])
        kbar = outs[4].reshape(batch, seq // MOBA_BLOCK, SELF_AUG) if is_moba else None
        o_self = _attention(qa, ka, va, kbar, causal=True, moba=is_moba, tq=1024)
        o_mem = _attention(qm, mk, mv, None, causal=False, moba=False, tq=512)

        rw = pad_lanes(router_w[i])
        rwh = rw.astype(_MXU)
        rwl = (rw - rwh.astype(_F32)).astype(_MXU)
        rb = jnp.pad(router_b[i][None, :], ((0, 0), (0, LANES - N_EXPERTS)), constant_values=MASK_VALUE)
        h1, ti, tg, rk, cnt = _post(h, o_self.reshape(n, SELF_WIDTH), o_mem.reshape(n, MEM_WIDTH),
                                    w_o[i][:SELF_WIDTH].astype(_MXU), w_o[i][SELF_WIDTH:].astype(_MXU),
                                    ln1_g[i][None, :], ln1_b[i][None, :], rwh, rwl, rb, alpha)

        pos, gend, gsz, blk_e, nvalid = _route(ti[:, :TOP_K], rk[:, :TOP_K], cnt[-1, 0, :N_EXPERTS], bm, nblk)
        pos_t = pos.reshape(n // tm_c, tm_c, TOP_K).transpose(0, 2, 1).reshape(n // tm_c, 1, TOP_K * tm_c)
        x_sorted = _dispatch(gend, gsz, pos_t, h1, nblk * bm, tm=tm_c, bm=bm)
        y_sorted = _moe_experts(blk_e, nvalid, x_sorted, w_gate_up, b_gate_up4, w_down, b_down4, i, bm=bm)
        h = _combine(pos_t, h1, tg, y_sorted, ln2_g[i][None, :], ln2_b[i][None, :], alpha, tm=tm_c)
    return h.reshape(batch, seq, d)
```

```python
import functools
import math

import numpy as np
import jax
import jax.numpy as jnp
from jax import lax
from jax.experimental import pallas as pl
from jax.experimental.pallas import tpu as pltpu

D_MODEL = 1024
HEAD_DIM = 64
N_SELF_HEADS = 12
N_MEM_HEADS = 4
SELF_WIDTH = N_SELF_HEADS * HEAD_DIM
MEM_WIDTH = N_MEM_HEADS * HEAD_DIM
IN_COLS = 3 * SELF_WIDTH + MEM_WIDTH
MOBA_BLOCK = 256
MOBA_TOPK = 3
N_EXPERTS = 32
TOP_K = 4
D_FF = D_MODEL
SWIGLU_LIMIT = 7.0
SWIGLU_ALPHA = 1.702
LN_EPS = 1e-5

LANES = 128
HEAD_BLOCK = LANES
SELF_AUG = N_SELF_HEADS * HEAD_BLOCK
MEM_AUG = N_MEM_HEADS * HEAD_BLOCK
AUX_ONES = 36
AUX_BLOCK0 = 40
BLOCK_LANE0 = 76
MAX_KEY_BLOCKS = 32
QK_SCALE = HEAD_DIM ** -0.5
LOG2E = math.log2(math.e)
MASK_VALUE = -1e30
VMEM_LIMIT = 56 * 1024 * 1024

_MXU = jnp.bfloat16
_F32 = jnp.float32


def _dot(a, b):
    return jnp.dot(a, b, preferred_element_type=_F32)


def _dot_nt(a, b):
    return lax.dot_general(a, b, (((1,), (1,)), ((), ())), preferred_element_type=_F32)


def _split3(x):
    p1 = x.astype(_MXU)
    r1 = x - p1.astype(_F32)
    p2 = r1.astype(_MXU)
    r2 = r1 - p2.astype(_F32)
    return p1, p2, r2.astype(_MXU)


def _layer_norm(y, g, b):
    mu = jnp.mean(y, axis=-1, keepdims=True)
    yc = y - mu
    var = jnp.mean(yc * yc, axis=-1, keepdims=True)
    return yc * lax.rsqrt(var + LN_EPS) * g + b


def _params(*sem):
    return pltpu.CompilerParams(dimension_semantics=sem, vmem_limit_bytes=VMEM_LIMIT)


def _const_spec(shape):
    nd = len(shape)
    return pl.BlockSpec(shape, lambda *_: (0,) * nd)


def _spread_heads(real, layout):
    lane = lax.broadcasted_iota(jnp.int32, (real.shape[0], LANES), 1)
    low = lane < HEAD_DIM
    blocks = []
    for p in range(real.shape[1] // LANES):
        pair = real[:, p * LANES:(p + 1) * LANES]
        blocks.append(jnp.where(low, pair, 0.0))
        if layout == "low":
            blocks.append(jnp.where(low, pltpu.roll(pair, HEAD_DIM, 1), 0.0))
        else:
            blocks.append(jnp.where(low, 0.0, pair))
    return jnp.concatenate(blocks, axis=1)


def _proj_kernel(*refs, outs, with_kbar):
    n_out = len(outs)
    x_ref, aux_ref, w_ref = refs[0], refs[1], refs[2]
    s_refs = refs[3:3 + n_out]
    o_refs = refs[3 + n_out:3 + 2 * n_out]
    real = _dot(x_ref[...].astype(_MXU), w_ref[...])
    aux = aux_ref[...]
    for i, (start, width, layout, has_aux) in enumerate(outs):
        acc = _spread_heads(real[:, start:start + width], layout)
        if has_aux:
            acc = acc + _dot(aux, s_refs[i][...])
        o_refs[i][...] = acc.astype(o_refs[i].dtype)
        if with_kbar and i == 1:
            kbar_ref = refs[3 + 2 * n_out]
            lane = lax.broadcasted_iota(jnp.int32, (1, acc.shape[1]), 1)
            kbar = jnp.mean(acc, axis=0, keepdims=True)
            kbar_ref[0] = jnp.where(lane % HEAD_BLOCK < HEAD_DIM, kbar, 0.0)


def _proj(x, aux, w, outs, ss, *, with_kbar=False, tm=256):
    m, kdim = x.shape
    ss = [s if s is not None else jnp.zeros((LANES, LANES), _MXU) for s in ss]
    widths = [2 * width for _, width, _, _ in outs]
    out_shape = [jax.ShapeDtypeStruct((m, wd), _MXU) for wd in widths]
    out_specs = [pl.BlockSpec((tm, wd), lambda i: (i, 0)) for wd in widths]
    if with_kbar:
        assert tm == MOBA_BLOCK
        out_shape.append(jax.ShapeDtypeStruct((m // tm, 1, widths[1]), _F32))
        out_specs.append(pl.BlockSpec((1, 1, widths[1]), lambda i: (i, 0, 0)))
    in_specs = [pl.BlockSpec((tm, kdim), lambda i: (i, 0)),
                pl.BlockSpec((tm, LANES), lambda i: (i, 0)),
                _const_spec(w.shape)]
    in_specs += [_const_spec(s.shape) for s in ss]
    return pl.pallas_call(
        functools.partial(_proj_kernel, outs=tuple(outs), with_kbar=with_kbar),
        out_shape=out_shape,
        grid=(m // tm,),
        in_specs=in_specs,
        out_specs=out_specs,
        compiler_params=_params("parallel"),
        name="proj",
    )(x, aux, w, *ss)


def _fgate_kernel(x_ref, w_ref, b_ref, tri_ref, place_ref, ones_ref, aux_ref, carry_ref):
    @pl.when(pl.program_id(1) == 0)
    def _():
        carry_ref[...] = jnp.zeros_like(carry_ref)

    z = _dot(x_ref[...].astype(_MXU), w_ref[...]) + b_ref[...]
    ls = jnp.minimum(z, 0.0) - jnp.log(1.0 + jnp.exp(-jnp.abs(z)))
    tri = tri_ref[...]
    l1, l2, l3 = _split3(ls)
    c = _dot(tri, l1) + _dot(tri, l2) + _dot(tri, l3) + carry_ref[...]
    carry_ref[...] = c[c.shape[0] - 1:c.shape[0], :]
    c1, c2, c3 = _split3(c * LOG2E)
    aux = _dot(c1, place_ref[0]) + _dot(c2, place_ref[1]) + _dot(c3, place_ref[2]) + ones_ref[...]
    aux_ref[...] = aux.astype(aux_ref.dtype)


def _fgate(x, w_fg, b_fg, batch, seq, *, tc=512):
    nt = seq // tc
    tri = jnp.asarray(np.tril(np.ones((tc, tc), np.float32)), _MXU)
    place = np.zeros((3, LANES, LANES), np.float32)
    for h in range(N_SELF_HEADS):
        for p in range(3):
            place[p, h, 3 * h + p] = 1.0
    ones = np.zeros((1, LANES), np.float32)
    ones[0, AUX_ONES] = 1.0
    return pl.pallas_call(
        _fgate_kernel,
        out_shape=jax.ShapeDtypeStruct((batch * seq, LANES), _MXU),
        grid=(batch, nt),
        in_specs=[pl.BlockSpec((tc, D_MODEL), lambda b, i: (b * nt + i, 0)),
                  _const_spec((D_MODEL, LANES)), _const_spec((1, LANES)),
                  _const_spec((tc, tc)), _const_spec((3, LANES, LANES)), _const_spec((1, LANES))],
        out_specs=pl.BlockSpec((tc, LANES), lambda b, i: (b * nt + i, 0)),
        scratch_shapes=[pltpu.VMEM((1, LANES), _F32)],
        compiler_params=_params("parallel", "arbitrary"),
        name="fgate",
    )(x, w_fg, b_fg, tri, jnp.asarray(place, _MXU), jnp.asarray(ones))


def _attn_kernel(*refs, tq, tkc, n_chunks, causal, moba):
    if moba:
        qa_ref, ka_ref, va_ref, kbar_ref, o_ref = refs
    else:
        qa_ref, ka_ref, va_ref, o_ref = refs
    qi = pl.program_id(2)
    lane = lax.broadcasted_iota(jnp.int32, (tq, HEAD_BLOCK), 1)
    col = lax.broadcasted_iota(jnp.int32, (tq, tkc), 1)
    own_blk_t = (qi * tq + lax.broadcasted_iota(jnp.int32, (1, tq), 1)) // MOBA_BLOCK

    heads = []
    for hh in range(2):
        cs = slice(hh * HEAD_BLOCK, (hh + 1) * HEAD_BLOCK)
        q = qa_ref[0, :, cs]
        if moba:
            nb = kbar_ref.shape[1]
            gate = _dot_nt(kbar_ref[0, :, cs], q.astype(_F32))
            blk = lax.broadcasted_iota(jnp.int32, (nb, tq), 0)
            g = jnp.where(blk < own_blk_t, gate, -jnp.inf)
            sel_t = jnp.zeros((nb, tq), _F32)
            for _ in range(MOBA_TOPK):
                mx = jnp.max(g, axis=0, keepdims=True)
                idx = jnp.min(jnp.where(g == mx, blk, nb), axis=0, keepdims=True)
                hit = blk == idx
                sel_t = jnp.where(hit & (mx > -jnp.inf), 1.0, sel_t)
                g = jnp.where(hit, -jnp.inf, g)
            barred = jnp.where(blk == own_blk_t, 0.0, 1.0 - sel_t).T
            place = (lax.broadcasted_iota(jnp.int32, (nb, HEAD_BLOCK), 1)
                     == lax.broadcasted_iota(jnp.int32, (nb, HEAD_BLOCK), 0) + BLOCK_LANE0)
            penalty = _dot(barred.astype(_MXU), jnp.where(place, 1.0, 0.0).astype(_MXU)) * MASK_VALUE
            q = (q.astype(_F32) + penalty).astype(_MXU)
        heads.append((cs, q))

    def scores(head, c):
        cs, q = head
        return _dot_nt(q, ka_ref[0, pl.ds(pl.multiple_of(c * tkc, tkc), tkc), cs])

    def values(head, c):
        return va_ref[0, pl.ds(pl.multiple_of(c * tkc, tkc), tkc), head[0]]

    def online(m, acc, s, v):
        m_new = jnp.maximum(m, jnp.max(s, axis=1, keepdims=True))
        p = jnp.exp2(s - m_new).astype(_MXU)
        return m_new, jnp.exp2(m - m_new) * acc + _dot(p, v)

    first = (qi * tq) // tkc if causal else 0
    carry = []
    for head in heads:
        s = scores(head, first)
        if causal:
            row = lax.broadcasted_iota(jnp.int32, (tq, tkc), 0)
            s = jnp.where(first * tkc + col <= qi * tq + row, s, MASK_VALUE)
        m0 = jnp.max(s, axis=1, keepdims=True)
        carry += [m0, _dot(jnp.exp2(s - m0).astype(_MXU), values(head, first))]

    def step(c, carry):
        out = []
        for hh, head in enumerate(heads):
            out += list(online(carry[2 * hh], carry[2 * hh + 1], scores(head, c), values(head, c)))
        return tuple(out)

    if causal:
        pairs = first // 2
        carry = lax.fori_loop(0, pairs, lambda j, c: step(2 * j + 1, step(2 * j, c)), tuple(carry))
        carry = lax.fori_loop(2 * pairs, first, step, carry)
    else:
        carry = lax.fori_loop(1, n_chunks, step, tuple(carry))
    acc_even, acc_odd = carry[1], carry[3]

    l_even = jnp.sum(jnp.where(lane == HEAD_DIM, acc_even, 0.0), axis=1, keepdims=True)
    l_odd = jnp.sum(jnp.where(lane == 0, acc_odd, 0.0), axis=1, keepdims=True)
    out = jnp.where(lane < HEAD_DIM, acc_even / l_even, acc_odd / l_odd)
    o_ref[0] = out.astype(o_ref.dtype)


def _attention(qa, ka, va, kbar, *, causal, moba, tq=256, tkc=1024):
    batch, seq, width = qa.shape
    seq_k = ka.shape[1]
    tkc = min(tkc, seq_k)
    npairs = width // (2 * HEAD_BLOCK)
    assert seq % tq == 0 and seq_k % tkc == 0 and tkc % MOBA_BLOCK == 0
    if causal:
        assert seq == seq_k and tkc % tq == 0
    if moba:
        assert tq % MOBA_BLOCK == 0
    pair = 2 * HEAD_BLOCK
    in_specs = [pl.BlockSpec((1, tq, pair), lambda b, p, i: (b, i, p)),
                pl.BlockSpec((1, seq_k, pair), lambda b, p, i: (b, 0, p)),
                pl.BlockSpec((1, seq_k, pair), lambda b, p, i: (b, 0, p))]
    args = [qa, ka, va]
    if moba:
        in_specs.append(pl.BlockSpec((1, kbar.shape[1], pair), lambda b, p, i: (b, 0, p)))
        args.append(kbar)
    return pl.pallas_call(
        functools.partial(_attn_kernel, tq=tq, tkc=tkc, n_chunks=seq_k // tkc, causal=causal, moba=moba),
        out_shape=jax.ShapeDtypeStruct((batch, seq, npairs * HEAD_BLOCK), _MXU),
        grid=(batch, npairs, seq // tq),
        in_specs=in_specs,
        out_specs=pl.BlockSpec((1, tq, HEAD_BLOCK), lambda b, p, i: (b, i, p)),
        compiler_params=_params("parallel", "parallel", "arbitrary"),
        name="attn_moba" if moba else ("attn_fox" if causal else "attn_mem"),
    )(*args)


def _post_kernel(h_ref, os_ref, om_ref, wo1_ref, wo2_ref, g_ref, b_ref, rwh_ref, rwl_ref, rb_ref, tri_ref,
                 h1_ref, ti_ref, tg_ref, rk_ref, cnt_ref, carry_ref, *, alpha):
    @pl.when(pl.program_id(0) == 0)
    def _():
        carry_ref[...] = jnp.zeros_like(carry_ref)

    y = alpha * h_ref[...] + _dot(os_ref[...], wo1_ref[...]) + _dot(om_ref[...], wo2_ref[...])
    h1 = _layer_norm(y, g_ref[...], b_ref[...])
    h1_ref[...] = h1
    hi = h1.astype(_MXU)
    lo = (h1 - hi.astype(_F32)).astype(_MXU)
    rwh = rwh_ref[...]
    logits = _dot(hi, rwh) + _dot(hi, rwl_ref[...]) + _dot(lo, rwh) + rb_ref[...]
    lane = lax.broadcasted_iota(jnp.int32, logits.shape, 1)
    ti = jnp.zeros(logits.shape, jnp.int32)
    tg = jnp.zeros(logits.shape, _F32)
    chosen = jnp.zeros(logits.shape, _F32)
    top = None
    denom = None
    es, hits = [], []
    for k in range(TOP_K):
        mx = jnp.max(logits, axis=1, keepdims=True)
        idx = jnp.min(jnp.where(logits == mx, lane, LANES), axis=1, keepdims=True)
        hit = lane == idx
        hits.append(hit)
        logits = jnp.where(hit, -jnp.inf, logits)
        chosen = jnp.where(hit, 1.0, chosen)
        ti = jnp.where(lane == k, idx, ti)
        if k == 0:
            top = mx
        e = jnp.exp(mx - top)
        es.append(e)
        denom = e if denom is None else denom + e
    earlier = _dot(tri_ref[...], chosen.astype(_MXU)) + carry_ref[...]
    carry_ref[...] = carry_ref[...] + jnp.sum(chosen, axis=0, keepdims=True)
    rk = jnp.zeros(logits.shape, _F32)
    for k in range(TOP_K):
        tg = jnp.where(lane == k, es[k] / denom, tg)
        rank_k = jnp.sum(jnp.where(hits[k], earlier, 0.0), axis=1, keepdims=True)
        rk = jnp.where(lane == k, rank_k, rk)
    ti_ref[...] = ti
    tg_ref[...] = tg
    rk_ref[...] = rk.astype(jnp.int32)
    cnt_ref[0] = carry_ref[...].astype(jnp.int32)


def _post(h, o_self, o_mem, wo1, wo2, g, b, rwh, rwl, rb, alpha, *, tm=512):
    n = h.shape[0]
    row = lambda w: pl.BlockSpec((tm, w), lambda i: (i, 0))
    tri = jnp.asarray(np.tril(np.ones((tm, tm), np.float32), -1), _MXU)
    return pl.pallas_call(
        functools.partial(_post_kernel, alpha=alpha),
        out_shape=[jax.ShapeDtypeStruct((n, D_MODEL), _F32),
                   jax.ShapeDtypeStruct((n, LANES), jnp.int32),
                   jax.ShapeDtypeStruct((n, LANES), _F32),
                   jax.ShapeDtypeStruct((n, LANES), jnp.int32),
                   jax.ShapeDtypeStruct((n // tm, 1, LANES), jnp.int32)],
        grid=(n // tm,),
        in_specs=[row(D_MODEL), row(SELF_WIDTH), row(MEM_WIDTH),
                  _const_spec(wo1.shape), _const_spec(wo2.shape),
                  _const_spec((1, D_MODEL)), _const_spec((1, D_MODEL)),
                  _const_spec(rwh.shape), _const_spec(rwl.shape), _const_spec((1, LANES)),
                  _const_spec((tm, tm))],
        out_specs=[row(D_MODEL), row(LANES), row(LANES), row(LANES),
                   pl.BlockSpec((1, 1, LANES), lambda i: (i, 0, 0))],
        scratch_shapes=[pltpu.VMEM((1, LANES), _F32)],
        compiler_params=_params("arbitrary"),
        name="post_attn",
    )(h, o_self, o_mem, wo1, wo2, g, b, rwh, rwl, rb, tri)


def _load_indices(idx_vmem_ref, idx_smem, sem):
    to_smem = pltpu.make_async_copy(idx_vmem_ref, idx_smem, sem)
    to_smem.start()
    to_smem.wait()


def _dispatch_kernel(gend_ref, gsz_ref, pos_ref, h1_ref, xs_hbm, zeros, idx_smem, sem, *, tm, bm):
    @pl.when(pl.program_id(0) == 0)
    def _():
        zeros[...] = jnp.zeros_like(zeros)

        def fill(start):
            return pltpu.make_async_copy(zeros, xs_hbm.at[pl.ds(pl.multiple_of(start, bm), bm), :], sem.at[2])

        for e in range(N_EXPERTS):
            pl.when(gsz_ref[e] > 0)(lambda e=e: fill(gend_ref[e] - bm).start())
        for e in range(N_EXPERTS):
            pl.when(gsz_ref[e] > 0)(lambda e=e: fill(gend_ref[e] - bm).wait())

        first_unused = gend_ref[N_EXPERTS - 1] // bm
        n_blocks = xs_hbm.shape[0] // bm

        def start_fill(b, carry):
            fill(b * bm).start()
            return carry

        def wait_fill(b, carry):
            fill(b * bm).wait()
            return carry

        lax.fori_loop(first_unused, n_blocks, start_fill, 0)
        lax.fori_loop(first_unused, n_blocks, wait_fill, 0)

    _load_indices(pos_ref.at[0], idx_smem, sem.at[0])
    for r in range(TOP_K * tm):
        p = idx_smem[0, r]
        row_copy = pltpu.make_async_copy(h1_ref.at[pl.ds(r % tm, 1), :], xs_hbm.at[pl.ds(p, 1), :], sem.at[1])
        row_copy.start(priority=r % 2)
    for k in range(TOP_K):
        pltpu.make_async_copy(h1_ref, xs_hbm.at[pl.ds(0, tm), :], sem.at[1]).wait()


def _dispatch(gend, gsz, pos_t, h1, n_rows, *, tm, bm):
    n = h1.shape[0]
    grid_spec = pltpu.PrefetchScalarGridSpec(
        num_scalar_prefetch=2,
        grid=(n // tm,),
        in_specs=[pl.BlockSpec((1, 1, TOP_K * tm), lambda i, ge, gs: (i, 0, 0)),
                  pl.BlockSpec((tm, D_MODEL), lambda i, ge, gs: (i, 0))],
        out_specs=pl.BlockSpec(memory_space=pl.ANY),
        scratch_shapes=[pltpu.VMEM((bm, D_MODEL), _F32),
                        pltpu.SMEM((1, TOP_K * tm), jnp.int32),
                        pltpu.SemaphoreType.DMA((3,))],
    )
    return pl.pallas_call(
        functools.partial(_dispatch_kernel, tm=tm, bm=bm),
        out_shape=jax.ShapeDtypeStruct((n_rows, D_MODEL), _F32),
        grid_spec=grid_spec,
        compiler_params=_params("arbitrary"),
        name="moe_dispatch",
    )(gend, gsz, pos_t, h1)


def _combine_kernel(pos_ref, h1_ref, tg_ref, y_hbm, g_ref, b_ref, o_ref, ybuf, idx_smem, sem, *, tm, alpha):
    _load_indices(pos_ref.at[0], idx_smem, sem.at[0])

    for r in range(TOP_K * tm):
        p = idx_smem[0, r]
        pltpu.make_async_copy(y_hbm.at[pl.ds(p, 1), :], ybuf.at[pl.ds(r, 1), :], sem.at[1]).start(priority=r % 2)
    pltpu.make_async_copy(y_hbm.at[pl.ds(0, TOP_K * tm), :], ybuf, sem.at[1]).wait()
    tg = tg_ref[...]
    f = alpha * h1_ref[...]
    for k in range(TOP_K):
        f = f + tg[:, k:k + 1] * ybuf[k * tm:(k + 1) * tm, :]
    o_ref[...] = _layer_norm(f, g_ref[...], b_ref[...])


def _combine(pos_t, h1, tg, y_sorted, g, b, alpha, *, tm):
    n = h1.shape[0]
    return pl.pallas_call(
        functools.partial(_combine_kernel, tm=tm, alpha=alpha),
        out_shape=jax.ShapeDtypeStruct((n, D_MODEL), _F32),
        grid=(n // tm,),
        in_specs=[pl.BlockSpec((1, 1, TOP_K * tm), lambda i: (i, 0, 0)),
                  pl.BlockSpec((tm, D_MODEL), lambda i: (i, 0)),
                  pl.BlockSpec((tm, LANES), lambda i: (i, 0)),
                  pl.BlockSpec(memory_space=pl.ANY),
                  _const_spec((1, D_MODEL)), _const_spec((1, D_MODEL))],
        out_specs=pl.BlockSpec((tm, D_MODEL), lambda i: (i, 0)),
        scratch_shapes=[pltpu.VMEM((TOP_K * tm, D_MODEL), _F32),
                        pltpu.SMEM((1, TOP_K * tm), jnp.int32),
                        pltpu.SemaphoreType.DMA((2,))],
        compiler_params=_params("arbitrary"),
        name="moe_combine",
    )(pos_t, h1, tg, y_sorted, g, b)


def _moe_kernel(be_ref, nv_ref, x_ref, wgu_ref, bgu_ref, wdn_ref, bdn_ref, y_ref, wgu_mxu, wdn_mxu):
    i = pl.program_id(0)
    valid = i < nv_ref[0]
    new_expert = jnp.logical_or(i == 0, be_ref[i] != be_ref[jnp.maximum(i - 1, 0)])

    @pl.when(jnp.logical_and(valid, new_expert))
    def _():
        wgu_mxu[...] = wgu_ref[0, 0].astype(_MXU)
        wdn_mxu[...] = wdn_ref[0, 0].astype(_MXU)

    @pl.when(valid)
    def _():
        gu = _dot(x_ref[...].astype(_MXU), wgu_mxu[...]) + bgu_ref[0, 0]
        g = jnp.minimum(gu[:, :D_FF], SWIGLU_LIMIT)
        u = jnp.clip(gu[:, D_FF:], -SWIGLU_LIMIT, SWIGLU_LIMIT)
        a = g * jax.nn.sigmoid(SWIGLU_ALPHA * g) * (u + 1.0)
        y_ref[...] = _dot(a.astype(_MXU), wdn_mxu[...]) + bdn_ref[0, 0]

    @pl.when(jnp.logical_not(valid))
    def _():
        y_ref[...] = jnp.zeros_like(y_ref)


def _moe_experts(blk_e, nvalid, x_sorted, wgu, bgu, wdn, bdn, layer, *, bm):
    nblk = x_sorted.shape[0] // bm
    grid_spec = pltpu.PrefetchScalarGridSpec(
        num_scalar_prefetch=2,
        grid=(nblk,),
        in_specs=[pl.BlockSpec((bm, D_MODEL), lambda i, be, nv: (jnp.minimum(i, nv[0] - 1), 0)),
                  pl.BlockSpec((1, 1, D_MODEL, 2 * D_FF), lambda i, be, nv: (layer, be[i], 0, 0)),
                  pl.BlockSpec((1, 1, 1, 2 * D_FF), lambda i, be, nv: (layer, be[i], 0, 0)),
                  pl.BlockSpec((1, 1, D_FF, D_MODEL), lambda i, be, nv: (layer, be[i], 0, 0)),
                  pl.BlockSpec((1, 1, 1, D_MODEL), lambda i, be, nv: (layer, be[i], 0, 0))],
        out_specs=pl.BlockSpec((bm, D_MODEL), lambda i, be, nv: (i, 0)),
        scratch_shapes=[pltpu.VMEM((D_MODEL, 2 * D_FF), _MXU),
                        pltpu.VMEM((D_FF, D_MODEL), _MXU)],
    )
    return pl.pallas_call(
        _moe_kernel,
        out_shape=jax.ShapeDtypeStruct((nblk * bm, D_MODEL), _F32),
        grid_spec=grid_spec,
        compiler_params=_params("arbitrary"),
        name="moe_experts",
    )(blk_e, nvalid, x_sorted, wgu, bgu, wdn, bdn)


def _alibi_slopes(n):
    def pow2(m):
        start = 2.0 ** (-(2.0 ** -(math.log2(m) - 3)))
        return [start * start ** i for i in range(m)]
    if math.log2(n).is_integer():
        s = pow2(n)
    else:
        c = 2 ** math.floor(math.log2(n))
        s = pow2(c) + pow2(2 * c)[0::2][: n - c]
    return np.array(s, dtype=np.float32)


def _np_split3(x):
    as_mxu = lambda a: np.asarray(a, np.float32).astype(jnp.bfloat16).astype(np.float32)
    p1 = as_mxu(x)
    p2 = as_mxu(x - p1)
    p3 = as_mxu(x - p1 - p2)
    return p1, p2, p3


def _value_ones(n_heads):
    s = np.zeros((LANES, n_heads * HEAD_BLOCK), np.float32)
    for h in range(n_heads):
        s[AUX_ONES, h * HEAD_BLOCK + (HEAD_DIM if h % 2 == 0 else 0)] = 1.0
    return s


def _fox_bias_placement():
    sq = np.zeros((LANES, SELF_AUG), np.float32)
    sk = np.zeros((LANES, SELF_AUG), np.float32)
    for h in range(N_SELF_HEADS):
        base = h * HEAD_BLOCK + HEAD_DIM
        for p in range(3):
            sq[3 * h + p, base + p] = 1.0
            sk[AUX_ONES, base + p] = 1.0
            sq[AUX_ONES, base + 3 + p] = 1.0
            sk[3 * h + p, base + 3 + p] = -1.0
    return sq, sk


def _moba_bias_placement():
    slopes = _np_split3(_alibi_slopes(N_SELF_HEADS) * np.float32(LOG2E))
    sq = np.zeros((LANES, SELF_AUG), np.float32)
    sk = np.zeros((LANES, SELF_AUG), np.float32)
    for h in range(N_SELF_HEADS):
        base = h * HEAD_BLOCK + HEAD_DIM
        for tpart in range(2):
            for p in range(3):
                c = base + 3 * tpart + p
                sq[tpart, c] = 1.0
                sk[AUX_ONES, c] = -slopes[p][h]
                sq[AUX_ONES, c + 6] = slopes[p][h]
                sk[tpart, c + 6] = 1.0
        for n in range(MAX_KEY_BLOCKS):
            sk[AUX_BLOCK0 + n, h * HEAD_BLOCK + BLOCK_LANE0 + n] = 1.0
    return sq, sk


def _moba_aux(batch, seq):
    t = np.arange(seq)
    assert seq // MOBA_BLOCK <= MAX_KEY_BLOCKS
    aux = np.zeros((seq, LANES), np.float32)
    aux[:, 0] = (t // 64) * 64
    aux[:, 1] = t % 64
    aux[:, AUX_ONES] = 1.0
    aux[t, AUX_BLOCK0 + t // MOBA_BLOCK] = 1.0
    return jnp.asarray(np.tile(aux, (batch, 1)), _MXU)


def _route(top_i, rank, counts, bm, nblk):
    gsz = ((counts + bm - 1) // bm) * bm
    gend = jnp.cumsum(gsz)
    gstart = gend - gsz
    onehot = top_i[:, :, None] == jnp.arange(N_EXPERTS, dtype=jnp.int32)[None, None, :]
    pos = jnp.sum(jnp.where(onehot, gstart[None, None, :], 0), axis=-1) + rank
    blk_start = jnp.arange(nblk, dtype=jnp.int32) * bm
    blk_e = jnp.sum((blk_start[:, None] >= gend[None, :]).astype(jnp.int32), axis=1)
    blk_e = jnp.minimum(blk_e, N_EXPERTS - 1).astype(jnp.int32)
    nvalid = (gend[-1:] // bm).astype(jnp.int32)
    return pos, gend.astype(jnp.int32), gsz.astype(jnp.int32), blk_e, nvalid


def kernel(x, mem, w_in_moba, w_in_fox, b_fgate, w_mem_kv, w_o, ln1_g, ln1_b, router_w, router_b,
           w_gate_up, b_gate_up, w_down, b_down, ln2_g, ln2_b):
    batch, seq, d = x.shape
    depth = w_o.shape[0]
    n = batch * seq
    n_mem = mem.shape[1]
    alpha = (2 * depth) ** 0.25
    bm = 512
    nblk = (n * TOP_K) // bm + N_EXPERTS
    tm_c = 512

    ones_aux = np.zeros((batch * n_mem, LANES), np.float32)
    ones_aux[:, AUX_ONES] = 1.0
    mk, mv = _proj(mem.reshape(batch * n_mem, d), jnp.asarray(ones_aux, _MXU), w_mem_kv.astype(_MXU),
                   [(0, MEM_WIDTH, "low", False), (MEM_WIDTH, MEM_WIDTH, "alternating", True)],
                   [None, jnp.asarray(_value_ones(N_MEM_HEADS), _MXU)])
    mk = mk.reshape(batch, n_mem, MEM_AUG)
    mv = mv.reshape(batch, n_mem, MEM_AUG)

    sv = jnp.asarray(_value_ones(N_SELF_HEADS), _MXU)
    moba_aux = _moba_aux(batch, seq)
    moba_s = [jnp.asarray(s, _MXU) for s in _moba_bias_placement()]
    fox_s = [jnp.asarray(s, _MXU) for s in _fox_bias_placement()]
    pad_lanes = lambda w: jnp.pad(w, ((0, 0), (0, LANES - w.shape[1])))
    col_scale = np.ones((1, IN_COLS), np.float32)
    col_scale[0, :SELF_WIDTH] = QK_SCALE * LOG2E
    col_scale[0, 3 * SELF_WIDTH:] = QK_SCALE * LOG2E
    proj_outs = [(0, SELF_WIDTH, "low", True), (SELF_WIDTH, SELF_WIDTH, "low", True),
                 (2 * SELF_WIDTH, SELF_WIDTH, "alternating", True), (3 * SELF_WIDTH, MEM_WIDTH, "low", False)]
    b_gate_up4 = b_gate_up[:, :, None, :]
    b_down4 = b_down[:, :, None, :]

    h = x.reshape(n, d)
    for i in range(depth):
        j = i // 2
        is_moba = i % 2 == 0
        w_in = w_in_moba[j] if is_moba else w_in_fox[j]
        w_qkv = (w_in[:, :IN_COLS] * col_scale).astype(_MXU)
        if is_moba:
            aux = moba_aux
            sq, sk = moba_s
        else:
            aux = _fgate(h, pad_lanes(w_in[:, IN_COLS:]).astype(_MXU), pad_lanes(b_fgate[j][None, :]),
                         batch, seq)
            sq, sk = fox_s
        outs = _proj(h, aux, w_qkv, proj_outs, [sq, sk, sv, None], with_kbar=is_moba)
        qa, ka, va, qm = (o.reshape(batch, seq, -1) for o in outs[:4])
        kbar = outs[4].reshape(batch, seq // MOBA_BLOCK, SELF_AUG) if is_moba else None
        o_self = _attention(qa, ka, va, kbar, causal=True, moba=is_moba, tq=1024)
        o_mem = _attention(qm, mk, mv, None, causal=False, moba=False, tq=512)

        rw = pad_lanes(router_w[i])
        rwh = rw.astype(_MXU)
        rwl = (rw - rwh.astype(_F32)).astype(_MXU)
        rb = jnp.pad(router_b[i][None, :], ((0, 0), (0, LANES - N_EXPERTS)), constant_values=MASK_VALUE)
        h1, ti, tg, rk, cnt = _post(h, o_self.reshape(n, SELF_WIDTH), o_mem.reshape(n, MEM_WIDTH),
                                    w_o[i][:SELF_WIDTH].astype(_MXU), w_o[i][SELF_WIDTH:].astype(_MXU),
                                    ln1_g[i][None, :], ln1_b[i][None, :], rwh, rwl, rb, alpha)

        pos, gend, gsz, blk_e, nvalid = _route(ti[:, :TOP_K], rk[:, :TOP_K], cnt[-1, 0, :N_EXPERTS], bm, nblk)
        pos_t = pos.reshape(n // tm_c, tm_c, TOP_K).transpose(0, 2, 1).reshape(n // tm_c, 1, TOP_K * tm_c)
        x_sorted = _dispatch(gend, gsz, pos_t, h1, nblk * bm, tm=tm_c, bm=bm)
        y_sorted = _moe_experts(blk_e, nvalid, x_sorted, w_gate_up, b_gate_up4, w_down, b_down4, i, bm=bm)
        h = _combine(pos_t, h1, tg, y_sorted, ln2_g[i][None, :], ln2_b[i][None, :], alpha, tm=tm_c)
    return h.reshape(batch, seq, d)
```
